```python
import jax, jax.numpy as jnp
from jax import lax
import numpy as np

D_MODEL = 2048
BATCH = 1
SEQ = 8192
DEPTH = 2

MEM_LEN = 256
N_MIXERS = 2
D_FF = 5632
CHUNK = 128
GMLP_WIDTH = 2048
GMLP_GROUPS = 8
GMLP_GROUP_DIM = GMLP_WIDTH // GMLP_GROUPS
CONV_WIDTH = 3
XATTN_HEADS = 4
XATTN_HEAD_DIM = D_MODEL // XATTN_HEADS
RMS_EPS = 1e-6
LN_EPS = 1e-5

kernel_name = "hybrid_gmlp_shortconv_macaron_memxattn"


def rmsnorm(x, g):
    xf = x.astype(jnp.float32)
    y = xf * lax.rsqrt(jnp.mean(xf * xf, axis=-1, keepdims=True) + RMS_EPS)
    return (y * g.astype(jnp.float32)).astype(x.dtype)


def layernorm(x, g, b):
    xf = x.astype(jnp.float32)
    mu = jnp.mean(xf, axis=-1, keepdims=True)
    xc = xf - mu
    var = jnp.mean(xc * xc, axis=-1, keepdims=True)
    y = xc * lax.rsqrt(var + LN_EPS) * g.astype(jnp.float32) + b.astype(jnp.float32)
    return y.astype(x.dtype)


def swiglu(h, w13, w2):
    gate, up = jnp.split(h @ w13, 2, axis=-1)
    return (jax.nn.silu(gate) * up) @ w2


def gmlp_mixer(h, w_in, ln_g, ln_b, w_s, b_s, w_out):
    bsz, seq, _ = h.shape
    z = jax.nn.gelu(h @ w_in, approximate=False)
    u, v = jnp.split(z, 2, axis=-1)
    v = layernorm(v, ln_g, ln_b)
    vc = v.reshape(bsz, seq // CHUNK, CHUNK, GMLP_GROUPS, GMLP_GROUP_DIM)
    causal = jnp.tril(jnp.ones((CHUNK, CHUNK), dtype=bool))
    w = jnp.where(causal[None], w_s, jnp.zeros_like(w_s)).astype(vc.dtype)
    f = jnp.einsum('gts,bcsge->bctge', w, vc) + b_s.T[:, :, None].astype(vc.dtype)
    return (u * f.reshape(bsz, seq, GMLP_WIDTH)) @ w_out


def short_conv_mixer(h, w_in, conv_w, w_out):
    d = h.shape[-1]
    gate_b, gate_c, val = jnp.split(h @ w_in, 3, axis=-1)
    z = gate_c * val
    kern = conv_w[:, None, :].astype(z.dtype)
    conv = lax.conv_general_dilated(
        z, kern, window_strides=(1,), padding=[(CONV_WIDTH - 1, 0)],
        dimension_numbers=('NWC', 'WIO', 'NWC'), feature_group_count=d)
    return (gate_b * conv) @ w_out


def mem_cross_attn(h, mem_n, wq, wkv, wo):
    bsz, seq, d = h.shape
    m = mem_n.shape[1]
    q = (h @ wq).reshape(bsz, seq, XATTN_HEADS, XATTN_HEAD_DIM)
    k, v = jnp.split(mem_n @ wkv, 2, axis=-1)
    k = k.reshape(bsz, m, XATTN_HEADS, XATTN_HEAD_DIM)
    v = v.reshape(bsz, m, XATTN_HEADS, XATTN_HEAD_DIM)
    s = jnp.einsum('bshd,bmhd->bhsm', q, k).astype(jnp.float32) * (XATTN_HEAD_DIM ** -0.5)
    p = jax.nn.softmax(s, axis=-1).astype(v.dtype)
    o = jnp.einsum('bhsm,bmhd->bshd', p, v).reshape(bsz, seq, d)
    return o @ wo


def setup_inputs(seed: int = 0) -> dict:
    key = jax.random.key(seed)
    ks = iter(jax.random.split(key, 32))
    n_a = (DEPTH + 1) // 2
    n_b = DEPTH // 2
    D, F, E = D_MODEL, D_FF, GMLP_WIDTH

    def w(shape, fan_in):
        return jax.random.normal(next(ks), shape, jnp.float32) * (fan_in ** -0.5)

    def gain(shape):
        return 1.0 + 0.02 * jax.random.normal(next(ks), shape, jnp.float32)

    def bias(shape):
        return 0.02 * jax.random.normal(next(ks), shape, jnp.float32)

    return {
        "x": jax.random.normal(next(ks), (BATCH, SEQ, D), jnp.float32),
        "mem": jax.random.normal(next(ks), (BATCH, MEM_LEN, D), jnp.float32),
        "ffn1_norm": gain((DEPTH, D)),
        "ffn1_w13": w((DEPTH, D, 2 * F), D),
        "ffn1_w2": w((DEPTH, F, D), F),
        "mix_norm": gain((DEPTH, D)),
        "gmlp_w_in": w((n_a, D, 2 * E), D),
        "gmlp_ln_g": gain((n_a, E)),
        "gmlp_ln_b": bias((n_a, E)),
        "gmlp_w_s": w((n_a, GMLP_GROUPS, CHUNK, CHUNK), CHUNK),
        "gmlp_b_s": gain((n_a, GMLP_GROUPS, CHUNK)),
        "gmlp_w_out": w((n_a, E, D), E),
        "conv_w_in": w((n_b, D, 3 * D), D),
        "conv_w": w((n_b, CONV_WIDTH, D), CONV_WIDTH),
        "conv_w_out": w((n_b, D, D), D),
        "xattn_norm": gain((DEPTH, D)),
        "mem_norm": gain((DEPTH, D)),
        "xattn_wq": w((DEPTH, D, D), D),
        "xattn_wkv": w((DEPTH, D, 2 * D), D),
        "xattn_wo": w((DEPTH, D, D), D),
        "ffn2_norm": gain((DEPTH, D)),
        "ffn2_w13": w((DEPTH, D, 2 * F), D),
        "ffn2_w2": w((DEPTH, F, D), F),
        "final_norm": gain((D,)),
    }


def reference(x, mem, ffn1_norm, ffn1_w13, ffn1_w2, mix_norm,
              gmlp_w_in, gmlp_ln_g, gmlp_ln_b, gmlp_w_s, gmlp_b_s, gmlp_w_out,
              conv_w_in, conv_w, conv_w_out,
              xattn_norm, mem_norm, xattn_wq, xattn_wkv, xattn_wo,
              ffn2_norm, ffn2_w13, ffn2_w2, final_norm):
    for i in range(DEPTH):
        x = x + 0.5 * swiglu(rmsnorm(x, ffn1_norm[i]), ffn1_w13[i], ffn1_w2[i])
        h = rmsnorm(x, mix_norm[i])
        j = i // N_MIXERS
        if i % N_MIXERS == 0:
            x = x + gmlp_mixer(h, gmlp_w_in[j], gmlp_ln_g[j], gmlp_ln_b[j],
                               gmlp_w_s[j], gmlp_b_s[j], gmlp_w_out[j])
        else:
            x = x + short_conv_mixer(h, conv_w_in[j], conv_w[j], conv_w_out[j])
        x = x + mem_cross_attn(rmsnorm(x, xattn_norm[i]), rmsnorm(mem, mem_norm[i]),
                               xattn_wq[i], xattn_wkv[i], xattn_wo[i])
        x = x + 0.5 * swiglu(rmsnorm(x, ffn2_norm[i]), ffn2_w13[i], ffn2_w2[i])
    return rmsnorm(x, final_norm)
```

```python
import functools

import jax
import jax.numpy as jnp
from jax import lax
from jax.experimental import pallas as pl
from jax.experimental.pallas import tpu as pltpu

D_MODEL = 2048
SEQ = 8192
DEPTH = 2
MEM_LEN = 256
D_FF = 5632
CHUNK = 128
GMLP_WIDTH = 2048
GMLP_GROUPS = 8
GMLP_GROUP_DIM = GMLP_WIDTH // GMLP_GROUPS
CONV_WIDTH = 3
XATTN_HEADS = 4
XATTN_HEAD_DIM = D_MODEL // XATTN_HEADS
RMS_EPS = 1e-6
LN_EPS = 1e-5

F32 = jnp.float32
BF16 = jnp.bfloat16

V7X_VMEM_LIMIT_BYTES = 60 * 1024 * 1024
SUBLANES = 8

FFN_TM = 1024
FFN_TF = 512
MIX_TM = 256
CONV_TN = 512
KV_TN = 1024


def _rms(x, g):
    return x * lax.rsqrt(jnp.mean(x * x, axis=-1, keepdims=True) + RMS_EPS) * g


def _dot(a, b):
    return jnp.dot(a, b, preferred_element_type=F32)


def _params(semantics):
    return pltpu.CompilerParams(dimension_semantics=semantics,
                                vmem_limit_bytes=V7X_VMEM_LIMIT_BYTES)


def _resident(shape):
    return pl.BlockSpec(shape, lambda *_: (0,) * len(shape), pipeline_mode=pl.Buffered(1))


def _ffn_kernel(x_ref, g_ref, w1_ref, w3_ref, w2_ref, fg_ref, o_ref, h_ref, *, final_norm):
    j = pl.program_id(1)

    @pl.when(j == 0)
    def _():
        x = x_ref[...]
        h_ref[...] = _rms(x, g_ref[...]).astype(BF16)
        o_ref[...] = x

    h = h_ref[...]
    gate = _dot(h, w1_ref[...])
    up = _dot(h, w3_ref[...])
    act = (gate * (0.5 / (1.0 + jnp.exp(-gate))) * up).astype(BF16)
    o_ref[...] += _dot(act, w2_ref[...])

    if final_norm:
        @pl.when(j == pl.num_programs(1) - 1)
        def _():
            o_ref[...] = _rms(o_ref[...], fg_ref[...])


def _ffn(x, gain, w13, w2, final_gain=None):
    m, d = x.shape
    f = w2.shape[0]
    nf = f // FFN_TF
    final_norm = final_gain is not None
    if final_gain is None:
        final_gain = gain
    row = lambda i, j: (i, 0)
    return pl.pallas_call(
        functools.partial(_ffn_kernel, final_norm=final_norm),
        out_shape=jax.ShapeDtypeStruct((m, d), F32),
        grid=(m // FFN_TM, nf),
        in_specs=[
            pl.BlockSpec((FFN_TM, d), row),
            pl.BlockSpec((1, d), lambda i, j: (0, 0)),
            pl.BlockSpec((d, FFN_TF), lambda i, j: (0, j)),
            pl.BlockSpec((d, FFN_TF), lambda i, j: (0, j + nf)),
            pl.BlockSpec((FFN_TF, d), lambda i, j: (j, 0)),
            pl.BlockSpec((1, d), lambda i, j: (0, 0)),
        ],
        out_specs=pl.BlockSpec((FFN_TM, d), row),
        scratch_shapes=[pltpu.VMEM((FFN_TM, d), BF16)],
        compiler_params=_params(("parallel", "arbitrary")),
        name="ffn",
    )(x, gain, w13, w13, w2, final_gain)


def _gmlp_kernel(x_ref, g_ref, win_ref, lng_ref, lnb_ref, ws_ref, bs_ref, wout_ref, o_ref,
                 u_ref, v_ref, uf_ref):
    e = GMLP_WIDTH
    x = x_ref[...]
    h = _rms(x, g_ref[...]).astype(BF16)

    def gelu(z):
        return 0.5 * z * (1.0 + lax.erf(z * (2.0 ** -0.5)))

    u_ref[...] = gelu(_dot(h, win_ref[:, :e]))
    v = gelu(_dot(h, win_ref[:, e:]))
    mu = jnp.mean(v, axis=-1, keepdims=True)
    vc = v - mu
    var = jnp.mean(vc * vc, axis=-1, keepdims=True)
    v_ref[...] = (vc * lax.rsqrt(var + LN_EPS) * lng_ref[...] + lnb_ref[...]).astype(BF16)

    t_idx = lax.broadcasted_iota(jnp.int32, (CHUNK, CHUNK), 0)
    s_idx = lax.broadcasted_iota(jnp.int32, (CHUNK, CHUNK), 1)
    causal = t_idx >= s_idx
    for g in range(GMLP_GROUPS):
        w = jnp.where(causal, ws_ref[g], 0.0).astype(BF16)
        cols = slice(g * GMLP_GROUP_DIM, (g + 1) * GMLP_GROUP_DIM)
        for c in range(MIX_TM // CHUNK):
            rows = slice(c * CHUNK, (c + 1) * CHUNK)
            f = _dot(w, v_ref[rows, cols]) + bs_ref[g]
            uf_ref[rows, cols] = (u_ref[rows, cols] * f).astype(BF16)
    o_ref[...] = x + _dot(uf_ref[...], wout_ref[...])


def _gmlp(x, gain, w_in, ln_g, ln_b, w_s, b_s, w_out):
    m, d = x.shape
    e = GMLP_WIDTH
    row = lambda i: (i, 0)
    bias = jnp.broadcast_to(b_s[:, :, None], (GMLP_GROUPS, CHUNK, GMLP_GROUP_DIM))
    return pl.pallas_call(
        _gmlp_kernel,
        out_shape=jax.ShapeDtypeStruct((m, d), F32),
        grid=(m // MIX_TM,),
        in_specs=[
            pl.BlockSpec((MIX_TM, d), row),
            _resident((1, d)),
            _resident((d, 2 * e)),
            _resident((1, e)),
            _resident((1, e)),
            _resident((GMLP_GROUPS, CHUNK, CHUNK)),
            _resident((GMLP_GROUPS, CHUNK, GMLP_GROUP_DIM)),
            _resident((e, d)),
        ],
        out_specs=pl.BlockSpec((MIX_TM, d), row),
        scratch_shapes=[pltpu.VMEM((MIX_TM, e), F32),
                        pltpu.VMEM((MIX_TM, e), BF16),
                        pltpu.VMEM((MIX_TM, e), BF16)],
        compiler_params=_params(("parallel",)),
        name="gmlp",
    )(x, gain, w_in, ln_g, ln_b, w_s, bias, w_out)


def _conv_kernel(x_ref, g_ref, win_ref, cw_ref, wout_ref, o_ref, z_ref, gb_ref):
    d = D_MODEL
    tm = MIX_TM
    i = pl.program_id(0)

    @pl.when(i == 0)
    def _():
        z_ref[0:SUBLANES, :] = jnp.zeros((SUBLANES, d), F32)

    @pl.when(i > 0)
    def _():
        z_ref[0:SUBLANES, :] = z_ref[tm:tm + SUBLANES, :]

    x = x_ref[...]
    h = _rms(x, g_ref[...]).astype(BF16)
    for jb in range(d // CONV_TN):
        lo = jb * CONV_TN
        cols = slice(lo, lo + CONV_TN)
        gate_b = _dot(h, win_ref[:, lo:lo + CONV_TN])
        gate_c = _dot(h, win_ref[:, d + lo:d + lo + CONV_TN])
        val = _dot(h, win_ref[:, 2 * d + lo:2 * d + lo + CONV_TN])
        z = gate_c * val
        z_ref[SUBLANES:SUBLANES + tm, cols] = z
        conv = (cw_ref[0:1, cols] * z_ref[SUBLANES - 2:SUBLANES - 2 + tm, cols]
                + cw_ref[1:2, cols] * z_ref[SUBLANES - 1:SUBLANES - 1 + tm, cols]
                + cw_ref[2:3, cols] * z)
        gb_ref[:, cols] = (gate_b * conv).astype(BF16)
    o_ref[...] = x + _dot(gb_ref[...], wout_ref[...])


def _short_conv(x, gain, w_in, conv_w, w_out):
    m, d = x.shape
    row = lambda i: (i, 0)
    return pl.pallas_call(
        _conv_kernel,
        out_shape=jax.ShapeDtypeStruct((m, d), F32),
        grid=(m // MIX_TM,),
        in_specs=[
            pl.BlockSpec((MIX_TM, d), row),
            _resident((1, d)),
            _resident((d, 3 * d)),
            _resident((CONV_WIDTH, d)),
            _resident((d, d)),
        ],
        out_specs=pl.BlockSpec((MIX_TM, d), row),
        scratch_shapes=[pltpu.VMEM((MIX_TM + SUBLANES, d), F32),
                        pltpu.VMEM((MIX_TM, d), BF16)],
        compiler_params=_params(("arbitrary",)),
        name="short_conv",
    )(x, gain, w_in, conv_w, w_out)


def _kv_kernel(mem_ref, g_ref, wkv_ref, kv_ref):
    mem_n = _rms(mem_ref[...], g_ref[...]).astype(BF16)
    kv_ref[...] = _dot(mem_n, wkv_ref[...]).astype(BF16)


def _kv_proj(mem, gain, wkv):
    mlen, d = mem.shape
    n = wkv.shape[1]
    return pl.pallas_call(
        _kv_kernel,
        out_shape=jax.ShapeDtypeStruct((mlen, n), BF16),
        grid=(n // KV_TN,),
        in_specs=[
            pl.BlockSpec((mlen, d), lambda j: (0, 0)),
            pl.BlockSpec((1, d), lambda j: (0, 0)),
            pl.BlockSpec((d, KV_TN), lambda j: (0, j)),
        ],
        out_specs=pl.BlockSpec((mlen, KV_TN), lambda j: (0, j)),
        compiler_params=_params(("parallel",)),
        name="kv_proj",
    )(mem, gain, wkv)


def _xattn_kernel(x_ref, g_ref, wq_ref, k_ref, v_ref, wo_ref, o_ref, q_ref, ob_ref):
    hd = XATTN_HEAD_DIM
    x = x_ref[...]
    h = _rms(x, g_ref[...]).astype(BF16)
    q_ref[...] = _dot(h, wq_ref[...]).astype(BF16)
    for hh in range(XATTN_HEADS):
        cols = slice(hh * hd, (hh + 1) * hd)
        s = lax.dot_general(q_ref[:, cols], k_ref[:, cols], (((1,), (1,)), ((), ())),
                            preferred_element_type=F32) * (hd ** -0.5)
        p = jnp.exp(s - jnp.max(s, axis=-1, keepdims=True))
        p = p / jnp.sum(p, axis=-1, keepdims=True)
        ob_ref[:, cols] = _dot(p.astype(BF16), v_ref[:, cols]).astype(BF16)
    o_ref[...] = x + _dot(ob_ref[...], wo_ref[...])


def _xattn(x, gain, wq, kv, wo):
    m, d = x.shape
    mlen = kv.shape[0]
    row = lambda i: (i, 0)
    return pl.pallas_call(
        _xattn_kernel,
        out_shape=jax.ShapeDtypeStruct((m, d), F32),
        grid=(m // MIX_TM,),
        in_specs=[
            pl.BlockSpec((MIX_TM, d), row),
            _resident((1, d)),
            _resident((d, d)),
            pl.BlockSpec((mlen, d), lambda i: (0, 0), pipeline_mode=pl.Buffered(1)),
            pl.BlockSpec((mlen, d), lambda i: (0, 1), pipeline_mode=pl.Buffered(1)),
            _resident((d, d)),
        ],
        out_specs=pl.BlockSpec((MIX_TM, d), row),
        scratch_shapes=[pltpu.VMEM((MIX_TM, d), BF16),
                        pltpu.VMEM((MIX_TM, d), BF16)],
        compiler_params=_params(("parallel",)),
        name="xattn",
    )(x, gain, wq, kv, kv, wo)


def kernel(x, mem, ffn1_norm, ffn1_w13, ffn1_w2, mix_norm, gmlp_w_in, gmlp_ln_g, gmlp_ln_b, gmlp_w_s, gmlp_b_s, gmlp_w_out, conv_w_in, conv_w, conv_w_out, xattn_norm, mem_norm, xattn_wq, xattn_wkv, xattn_wo, ffn2_norm, ffn2_w13, ffn2_w2, final_norm):
    bsz, seq, d = x.shape
    assert (bsz, seq, d) == (1, SEQ, D_MODEL), "conv carry assumes one sequence laid out along rows"
    xs = x.reshape(seq, d)
    mem2 = mem.reshape(MEM_LEN, d)
    bf = lambda w: w.astype(BF16)
    vec = lambda v: v.reshape(1, -1)

    for i in range(DEPTH):
        xs = _ffn(xs, vec(ffn1_norm[i]), bf(ffn1_w13[i]), bf(ffn1_w2[i]))
        j = i // 2
        if i % 2 == 0:
            xs = _gmlp(xs, vec(mix_norm[i]), bf(gmlp_w_in[j]), vec(gmlp_ln_g[j]), vec(gmlp_ln_b[j]),
                       gmlp_w_s[j], gmlp_b_s[j], bf(gmlp_w_out[j]))
        else:
            xs = _short_conv(xs, vec(mix_norm[i]), bf(conv_w_in[j]), conv_w[j], bf(conv_w_out[j]))
        kv = _kv_proj(mem2, vec(mem_norm[i]), bf(xattn_wkv[i]))
        xs = _xattn(xs, vec(xattn_norm[i]), bf(xattn_wq[i]), kv, bf(xattn_wo[i]))
        last = i == DEPTH - 1
        xs = _ffn(xs, vec(ffn2_norm[i]), bf(ffn2_w13[i]), bf(ffn2_w2[i]),
                  final_gain=vec(final_norm) if last else None)
    return xs.reshape(bsz, seq, d)
```

```python
import functools

import jax
import jax.numpy as jnp
from jax import lax
from jax.experimental import pallas as pl
from jax.experimental.pallas import tpu as pltpu

D_MODEL = 2048
SEQ = 8192
DEPTH = 2
MEM_LEN = 256
D_FF = 5632
CHUNK = 128
GMLP_WIDTH = 2048
GMLP_GROUPS = 8
GMLP_GROUP_DIM = GMLP_WIDTH // GMLP_GROUPS
CONV_WIDTH = 3
XATTN_HEADS = 4
XATTN_HEAD_DIM = D_MODEL // XATTN_HEADS
RMS_EPS = 1e-6
LN_EPS = 1e-5

F32 = jnp.float32
BF16 = jnp.bfloat16

V7X_VMEM_LIMIT_BYTES = 60 * 1024 * 1024
SUBLANES = 8

FFN_TM = 1024
FFN_TF = 512
FFN_TF_FIRST = 256
MIX_TM = 256
CONV_TN = 512
KV_TN = 512
STAGE_BYTES = 2 * 1024 * 1024


def _rms(x, g):
    return x * lax.rsqrt(jnp.mean(x * x, axis=-1, keepdims=True) + RMS_EPS) * g


def _dot(a, b):
    return jnp.dot(a, b, preferred_element_type=F32)


def _params(semantics):
    return pltpu.CompilerParams(dimension_semantics=semantics,
                                vmem_limit_bytes=V7X_VMEM_LIMIT_BYTES)


def _resident(shape):
    return pl.BlockSpec(shape, lambda *_: (0,) * len(shape), pipeline_mode=pl.Buffered(1))


_HBM = pl.BlockSpec(memory_space=pl.ANY)


def _stage_rows(w_shape):
    rows = STAGE_BYTES // (w_shape[1] * 4) // 16 * 16
    while w_shape[0] % rows:
        rows -= 16
    return rows


def _stage_scratch(w_shape):
    return [pltpu.VMEM(w_shape, BF16),
            pltpu.VMEM((2, _stage_rows(w_shape), w_shape[1]), F32),
            pltpu.SemaphoreType.DMA((2,))]


def _load_weight(w_hbm, w_vmem, stage, sem):
    rows = stage.shape[1]
    n = w_hbm.shape[0] // rows

    def copy(c, slot):
        return pltpu.make_async_copy(w_hbm.at[pl.ds(c * rows, rows)], stage.at[slot], sem.at[slot])

    copy(0, 0).start()

    def body(c, carry):
        slot = lax.rem(c, 2)

        @pl.when(c + 1 < n)
        def _():
            copy(c + 1, 1 - slot).start()

        copy(c, slot).wait()
        w_vmem[pl.ds(pl.multiple_of(c * rows, rows), rows), :] = stage[slot].astype(BF16)
        return carry

    lax.fori_loop(0, n, body, 0)


def _swiglu_step(h_ref, w1, w3, w2, o_ref):
    h = h_ref[...]
    gate = _dot(h, w1)
    up = _dot(h, w3)
    act = (gate * (0.5 / (1.0 + jnp.exp(-gate))) * up).astype(BF16)
    o_ref[...] += _dot(act, w2)


def _ffn_first_kernel(x_ref, g_ref, w1_ref, w3_ref, w2_ref, fg_ref, o_ref, w1b_ref, w3b_ref, w2b_ref,
                      h_ref, *, final_norm):
    j = pl.program_id(0)

    @pl.when(j == 0)
    def _():
        x = x_ref[...]
        h_ref[...] = _rms(x, g_ref[...]).astype(BF16)
        o_ref[...] = x

    w1 = w1_ref[...].astype(BF16)
    w3 = w3_ref[...].astype(BF16)
    w2 = w2_ref[...].astype(BF16)
    w1b_ref[...] = w1
    w3b_ref[...] = w3
    w2b_ref[...] = w2
    _swiglu_step(h_ref, w1, w3, w2, o_ref)

    if final_norm:
        @pl.when(j == pl.num_programs(0) - 1)
        def _():
            o_ref[...] = _rms(o_ref[...], fg_ref[...])


def _ffn_rest_kernel(x_ref, g_ref, w1_ref, w3_ref, w2_ref, fg_ref, y0_hbm, o_ref, h_ref, sem,
                     *, nf, final_norm):
    s = pl.program_id(0)
    j = lax.rem(s + (nf - 1), nf)

    @pl.when(s == 0)
    def _():
        copy = pltpu.make_async_copy(y0_hbm, o_ref, sem)
        copy.start()
        copy.wait()

    @pl.when(s > 0)
    def _():
        @pl.when(j == 0)
        def _():
            x = x_ref[...]
            h_ref[...] = _rms(x, g_ref[...]).astype(BF16)
            o_ref[...] = x

        _swiglu_step(h_ref, w1_ref[...], w3_ref[...], w2_ref[...], o_ref)

        if final_norm:
            @pl.when(j == nf - 1)
            def _():
                o_ref[...] = _rms(o_ref[...], fg_ref[...])


def _ffn(x, gain, w13, w2, layer, final_gain=None):
    m, d = x.shape
    f = w2.shape[1]
    final_norm = final_gain is not None
    if final_gain is None:
        final_gain = gain
    vec = pl.BlockSpec((1, d), lambda s: (0, 0))

    nf0 = f // FFN_TF_FIRST
    y0, w1b, w3b, w2b = pl.pallas_call(
        functools.partial(_ffn_first_kernel, final_norm=final_norm),
        out_shape=[jax.ShapeDtypeStruct((FFN_TM, d), F32),
                   jax.ShapeDtypeStruct((d, f), BF16),
                   jax.ShapeDtypeStruct((d, f), BF16),
                   jax.ShapeDtypeStruct((f, d), BF16)],
        grid=(nf0,),
        in_specs=[
            pl.BlockSpec((FFN_TM, d), lambda j: (0, 0), pipeline_mode=pl.Buffered(1)),
            vec,
            pl.BlockSpec((None, d, FFN_TF_FIRST), lambda j: (layer, 0, j)),
            pl.BlockSpec((None, d, FFN_TF_FIRST), lambda j: (layer, 0, j + nf0)),
            pl.BlockSpec((None, FFN_TF_FIRST, d), lambda j: (layer, j, 0)),
            vec,
        ],
        out_specs=[
            pl.BlockSpec((FFN_TM, d), lambda j: (0, 0)),
            pl.BlockSpec((d, FFN_TF_FIRST), lambda j: (0, j)),
            pl.BlockSpec((d, FFN_TF_FIRST), lambda j: (0, j)),
            pl.BlockSpec((FFN_TF_FIRST, d), lambda j: (j, 0)),
        ],
        scratch_shapes=[pltpu.VMEM((FFN_TM, d), BF16)],
        compiler_params=_params(("arbitrary",)),
        name="ffn_first",
    )(x, gain, w13, w13, w2, final_gain)

    nf = f // FFN_TF
    n_rest = m // FFN_TM - 1
    out_row = lambda s: ((s + (nf - 1)) // nf, 0)
    x_row = lambda s: (jnp.maximum((s + (nf - 1)) // nf, 1), 0)
    w_col = lambda s: jnp.where(s == 0, 0, lax.rem(s + (nf - 1), nf))
    return pl.pallas_call(
        functools.partial(_ffn_rest_kernel, nf=nf, final_norm=final_norm),
        out_shape=jax.ShapeDtypeStruct((m, d), F32),
        grid=(1 + n_rest * nf,),
        in_specs=[
            pl.BlockSpec((FFN_TM, d), x_row),
            vec,
            pl.BlockSpec((d, FFN_TF), lambda s: (0, w_col(s))),
            pl.BlockSpec((d, FFN_TF), lambda s: (0, w_col(s))),
            pl.BlockSpec((FFN_TF, d), lambda s: (w_col(s), 0)),
            vec,
            _HBM,
        ],
        out_specs=pl.BlockSpec((FFN_TM, d), out_row),
        scratch_shapes=[pltpu.VMEM((FFN_TM, d), BF16), pltpu.SemaphoreType.DMA(())],
        compiler_params=_params(("arbitrary",)),
        name="ffn_rest",
    )(x, gain, w1b, w3b, w2b, final_gain, y0)


def _gmlp_kernel(x_ref, g_ref, win_hbm, lng_ref, lnb_ref, ws_ref, bs_ref, wout_hbm, o_ref,
                 win_ref, win_stage, win_sem, wout_ref, wout_stage, wout_sem, u_ref, v_ref, uf_ref,
                 *, layer):
    e = GMLP_WIDTH

    @pl.when(pl.program_id(0) == 0)
    def _():
        _load_weight(win_hbm.at[layer], win_ref, win_stage, win_sem)
        _load_weight(wout_hbm.at[layer], wout_ref, wout_stage, wout_sem)

    x = x_ref[...]
    h = _rms(x, g_ref[...]).astype(BF16)

    def gelu(z):
        return 0.5 * z * (1.0 + lax.erf(z * (2.0 ** -0.5)))

    u_ref[...] = gelu(_dot(h, win_ref[:, :e]))
    v = gelu(_dot(h, win_ref[:, e:]))
    mu = jnp.mean(v, axis=-1, keepdims=True)
    vc = v - mu
    var = jnp.mean(vc * vc, axis=-1, keepdims=True)
    v_ref[...] = (vc * lax.rsqrt(var + LN_EPS) * lng_ref[...] + lnb_ref[...]).astype(BF16)

    t_idx = lax.broadcasted_iota(jnp.int32, (CHUNK, CHUNK), 0)
    s_idx = lax.broadcasted_iota(jnp.int32, (CHUNK, CHUNK), 1)
    causal = t_idx >= s_idx
    for g in range(GMLP_GROUPS):
        w = jnp.where(causal, ws_ref[g], 0.0).astype(BF16)
        cols = slice(g * GMLP_GROUP_DIM, (g + 1) * GMLP_GROUP_DIM)
        for c in range(MIX_TM // CHUNK):
            rows = slice(c * CHUNK, (c + 1) * CHUNK)
            f = _dot(w, v_ref[rows, cols]) + bs_ref[g]
            uf_ref[rows, cols] = (u_ref[rows, cols] * f).astype(BF16)
    o_ref[...] = x + _dot(uf_ref[...], wout_ref[...])


def _gmlp(x, gain, w_in, ln_g, ln_b, w_s, b_s, w_out, layer):
    m, d = x.shape
    e = GMLP_WIDTH
    row = lambda i: (i, 0)
    bias = jnp.broadcast_to(b_s[:, :, None], (GMLP_GROUPS, CHUNK, GMLP_GROUP_DIM))
    return pl.pallas_call(
        functools.partial(_gmlp_kernel, layer=layer),
        out_shape=jax.ShapeDtypeStruct((m, d), F32),
        grid=(m // MIX_TM,),
        in_specs=[
            pl.BlockSpec((MIX_TM, d), row),
            _resident((1, d)),
            _HBM,
            _resident((1, e)),
            _resident((1, e)),
            _resident((GMLP_GROUPS, CHUNK, CHUNK)),
            _resident((GMLP_GROUPS, CHUNK, GMLP_GROUP_DIM)),
            _HBM,
        ],
        out_specs=pl.BlockSpec((MIX_TM, d), row),
        scratch_shapes=[*_stage_scratch((d, 2 * e)),
                        *_stage_scratch((e, d)),
                        pltpu.VMEM((MIX_TM, e), F32),
                        pltpu.VMEM((MIX_TM, e), BF16),
                        pltpu.VMEM((MIX_TM, e), BF16)],
        compiler_params=_params(("arbitrary",)),
        name="gmlp",
    )(x, gain, w_in, ln_g, ln_b, w_s, bias, w_out)


def _conv_kernel(x_ref, g_ref, win_hbm, cw_ref, wout_hbm, o_ref,
                 win_ref, win_stage, win_sem, wout_ref, wout_stage, wout_sem, z_ref, gb_ref, *, layer):
    d = D_MODEL
    tm = MIX_TM
    i = pl.program_id(0)

    @pl.when(i == 0)
    def _():
        _load_weight(win_hbm.at[layer], win_ref, win_stage, win_sem)
        _load_weight(wout_hbm.at[layer], wout_ref, wout_stage, wout_sem)
        z_ref[0:SUBLANES, :] = jnp.zeros((SUBLANES, d), F32)

    @pl.when(i > 0)
    def _():
        z_ref[0:SUBLANES, :] = z_ref[tm:tm + SUBLANES, :]

    x = x_ref[...]
    h = _rms(x, g_ref[...]).astype(BF16)
    for jb in range(d // CONV_TN):
        lo = jb * CONV_TN
        cols = slice(lo, lo + CONV_TN)
        gate_b = _dot(h, win_ref[:, lo:lo + CONV_TN])
        gate_c = _dot(h, win_ref[:, d + lo:d + lo + CONV_TN])
        val = _dot(h, win_ref[:, 2 * d + lo:2 * d + lo + CONV_TN])
        z = gate_c * val
        z_ref[SUBLANES:SUBLANES + tm, cols] = z
        conv = (cw_ref[0:1, cols] * z_ref[SUBLANES - 2:SUBLANES - 2 + tm, cols]
                + cw_ref[1:2, cols] * z_ref[SUBLANES - 1:SUBLANES - 1 + tm, cols]
                + cw_ref[2:3, cols] * z)
        gb_ref[:, cols] = (gate_b * conv).astype(BF16)
    o_ref[...] = x + _dot(gb_ref[...], wout_ref[...])


def _short_conv(x, gain, w_in, conv_w, w_out, layer):
    m, d = x.shape
    row = lambda i: (i, 0)
    return pl.pallas_call(
        functools.partial(_conv_kernel, layer=layer),
        out_shape=jax.ShapeDtypeStruct((m, d), F32),
        grid=(m // MIX_TM,),
        in_specs=[
            pl.BlockSpec((MIX_TM, d), row),
            _resident((1, d)),
            _HBM,
            _resident((CONV_WIDTH, d)),
            _HBM,
        ],
        out_specs=pl.BlockSpec((MIX_TM, d), row),
        scratch_shapes=[*_stage_scratch((d, 3 * d)),
                        *_stage_scratch((d, d)),
                        pltpu.VMEM((MIX_TM + SUBLANES, d), F32),
                        pltpu.VMEM((MIX_TM, d), BF16)],
        compiler_params=_params(("arbitrary",)),
        name="short_conv",
    )(x, gain, w_in, conv_w, w_out)


def _kv_kernel(mem_ref, g_ref, wkv_ref, kv_ref):
    mem_n = _rms(mem_ref[...], g_ref[...]).astype(BF16)
    kv_ref[...] = _dot(mem_n, wkv_ref[...].astype(BF16)).astype(BF16)


def _kv_proj(mem, gain, wkv, layer):
    mlen, d = mem.shape
    n = wkv.shape[2]
    return pl.pallas_call(
        _kv_kernel,
        out_shape=jax.ShapeDtypeStruct((mlen, n), BF16),
        grid=(n // KV_TN,),
        in_specs=[
            pl.BlockSpec((mlen, d), lambda j: (0, 0)),
            pl.BlockSpec((1, d), lambda j: (0, 0)),
            pl.BlockSpec((None, d, KV_TN), lambda j: (layer, 0, j)),
        ],
        out_specs=pl.BlockSpec((mlen, KV_TN), lambda j: (0, j)),
        compiler_params=_params(("parallel",)),
        name="kv_proj",
    )(mem, gain, wkv)


def _xattn_kernel(x_ref, g_ref, wq_hbm, k_ref, v_ref, wo_hbm, o_ref,
                  wq_ref, wq_stage, wq_sem, wo_ref, wo_stage, wo_sem, q_ref, ob_ref, *, layer):
    hd = XATTN_HEAD_DIM

    @pl.when(pl.program_id(0) == 0)
    def _():
        _load_weight(wq_hbm.at[layer], wq_ref, wq_stage, wq_sem)
        _load_weight(wo_hbm.at[layer], wo_ref, wo_stage, wo_sem)

    x = x_ref[...]
    h = _rms(x, g_ref[...]).astype(BF16)
    q_ref[...] = _dot(h, wq_ref[...]).astype(BF16)
    for hh in range(XATTN_HEADS):
        cols = slice(hh * hd, (hh + 1) * hd)
        s = lax.dot_general(q_ref[:, cols], k_ref[:, cols], (((1,), (1,)), ((), ())),
                            preferred_element_type=F32) * (hd ** -0.5)
        p = jnp.exp(s - jnp.max(s, axis=-1, keepdims=True))
        p = p / jnp.sum(p, axis=-1, keepdims=True)
        ob_ref[:, cols] = _dot(p.astype(BF16), v_ref[:, cols]).astype(BF16)
    o_ref[...] = x + _dot(ob_ref[...], wo_ref[...])


def _xattn(x, gain, wq, kv, wo, layer):
    m, d = x.shape
    mlen = kv.shape[0]
    row = lambda i: (i, 0)
    return pl.pallas_call(
        functools.partial(_xattn_kernel, layer=layer),
        out_shape=jax.ShapeDtypeStruct((m, d), F32),
        grid=(m // MIX_TM,),
        in_specs=[
            pl.BlockSpec((MIX_TM, d), row),
            _resident((1, d)),
            _HBM,
            pl.BlockSpec((mlen, d), lambda i: (0, 0), pipeline_mode=pl.Buffered(1)),
            pl.BlockSpec((mlen, d), lambda i: (0, 1), pipeline_mode=pl.Buffered(1)),
            _HBM,
        ],
        out_specs=pl.BlockSpec((MIX_TM, d), row),
        scratch_shapes=[*_stage_scratch((d, d)),
                        *_stage_scratch((d, d)),
                        pltpu.VMEM((MIX_TM, d), BF16),
                        pltpu.VMEM((MIX_TM, d), BF16)],
        compiler_params=_params(("arbitrary",)),
        name="xattn",
    )(x, gain, wq, kv, kv, wo)


def kernel(x, mem, ffn1_norm, ffn1_w13, ffn1_w2, mix_norm, gmlp_w_in, gmlp_ln_g, gmlp_ln_b, gmlp_w_s, gmlp_b_s, gmlp_w_out, conv_w_in, conv_w, conv_w_out, xattn_norm, mem_norm, xattn_wq, xattn_wkv, xattn_wo, ffn2_norm, ffn2_w13, ffn2_w2, final_norm):
    bsz, seq, d = x.shape
    assert (bsz, seq, d) == (1, SEQ, D_MODEL), "conv carry assumes one sequence laid out along rows"
    xs = x.reshape(seq, d)
    mem2 = mem.reshape(MEM_LEN, d)
    vec = lambda v: v.reshape(1, -1)

    for i in range(DEPTH):
        xs = _ffn(xs, vec(ffn1_norm[i]), ffn1_w13, ffn1_w2, i)
        j = i // 2
        if i % 2 == 0:
            xs = _gmlp(xs, vec(mix_norm[i]), gmlp_w_in, vec(gmlp_ln_g[j]), vec(gmlp_ln_b[j]),
                       gmlp_w_s[j], gmlp_b_s[j], gmlp_w_out, j)
        else:
            xs = _short_conv(xs, vec(mix_norm[i]), conv_w_in, conv_w[j], conv_w_out, j)
        kv = _kv_proj(mem2, vec(mem_norm[i]), xattn_wkv, i)
        xs = _xattn(xs, vec(xattn_norm[i]), xattn_wq, kv, xattn_wo, i)
        last = i == DEPTH - 1
        xs = _ffn(xs, vec(ffn2_norm[i]), ffn2_w13, ffn2_w2, i,
                  final_gain=vec(final_norm) if last else None)
    return xs.reshape(bsz, seq, d)
```

```python
import functools

import jax
import jax.numpy as jnp
from jax import lax
from jax.experimental import pallas as pl
from jax.experimental.pallas import tpu as pltpu

D_MODEL = 2048
SEQ = 8192
DEPTH = 2
MEM_LEN = 256
D_FF = 5632
CHUNK = 128
GMLP_WIDTH = 2048
GMLP_GROUPS = 8
GMLP_GROUP_DIM = GMLP_WIDTH // GMLP_GROUPS
CONV_WIDTH = 3
XATTN_HEADS = 4
XATTN_HEAD_DIM = D_MODEL // XATTN_HEADS
RMS_EPS = 1e-6
LN_EPS = 1e-5

F32 = jnp.float32
BF16 = jnp.bfloat16

V7X_VMEM_LIMIT_BYTES = 60 * 1024 * 1024
SUBLANES = 8

FFN_TM = 1024
FFN_TF = 512
FFN_TF_FIRST = 256
FFN_FIRST_ROWS = 2 * FFN_TM
NORM_ROWS = 256
MIX_TM = 256
ATT_TM = 256
CONV_TN = 512
KV_TN = 512
STAGE_BYTES = 2 * 1024 * 1024


def _rms(x, g):
    return x * lax.rsqrt(jnp.mean(x * x, axis=-1, keepdims=True) + RMS_EPS) * g


def _dot(a, b):
    return jnp.dot(a, b, preferred_element_type=F32)


def _params(semantics):
    return pltpu.CompilerParams(dimension_semantics=semantics,
                                vmem_limit_bytes=V7X_VMEM_LIMIT_BYTES)


def _resident(shape):
    return pl.BlockSpec(shape, lambda *_: (0,) * len(shape), pipeline_mode=pl.Buffered(1))


_HBM = pl.BlockSpec(memory_space=pl.ANY)


def _stage_rows(w_shape):
    rows = STAGE_BYTES // (w_shape[1] * 4) // 16 * 16
    while w_shape[0] % rows:
        rows -= 16
    return rows


def _stage_scratch(w_shape):
    return [pltpu.VMEM(w_shape, BF16),
            pltpu.VMEM((2, _stage_rows(w_shape), w_shape[1]), F32),
            pltpu.SemaphoreType.DMA((2,))]


def _load_weight(w_hbm, w_vmem, stage, sem):
    rows = stage.shape[1]
    n = w_hbm.shape[0] // rows

    def copy(c, slot):
        return pltpu.make_async_copy(w_hbm.at[pl.ds(c * rows, rows)], stage.at[slot], sem.at[slot])

    copy(0, 0).start()

    def body(c, carry):
        slot = lax.rem(c, 2)

        @pl.when(c + 1 < n)
        def _():
            copy(c + 1, 1 - slot).start()

        copy(c, slot).wait()
        w_vmem[pl.ds(pl.multiple_of(c * rows, rows), rows), :] = stage[slot].astype(BF16)
        return carry

    lax.fori_loop(0, n, body, 0)


def _swiglu_step(h_ref, w1, w3, w2, o_ref):
    h = h_ref[...]
    gate = _dot(h, w1)
    up = _dot(h, w3)
    act = (gate * (0.5 / (1.0 + jnp.exp(-gate))) * up).astype(BF16)
    o_ref[...] += _dot(act, w2)


def _ffn_first_kernel(x_hbm, g_ref, w1_ref, w3_ref, w2_ref, fg_ref, o_ref, w1b_ref, w3b_ref, w2b_ref,
                      h_ref, sem, *, final_norm):
    j = pl.program_id(0)

    @pl.when(j == 0)
    def _():
        copy = pltpu.make_async_copy(x_hbm.at[pl.ds(0, FFN_FIRST_ROWS)], o_ref, sem)
        copy.start()
        copy.wait()
        for r in range(0, FFN_FIRST_ROWS, NORM_ROWS):
            h_ref[r:r + NORM_ROWS, :] = _rms(o_ref[r:r + NORM_ROWS, :], g_ref[...]).astype(BF16)

    w1 = w1_ref[...].astype(BF16)
    w3 = w3_ref[...].astype(BF16)
    w2 = w2_ref[...].astype(BF16)
    w1b_ref[...] = w1
    w3b_ref[...] = w3
    w2b_ref[...] = w2
    _swiglu_step(h_ref, w1, w3, w2, o_ref)

    if final_norm:
        @pl.when(j == pl.num_programs(0) - 1)
        def _():
            for r in range(0, FFN_FIRST_ROWS, NORM_ROWS):
                o_ref[r:r + NORM_ROWS, :] = _rms(o_ref[r:r + NORM_ROWS, :], fg_ref[...])


def _ffn_rest_kernel(x_ref, g_ref, w1_ref, w3_ref, w2_ref, fg_ref, y0_hbm, o_ref, h_ref, sem,
                     *, nf, n_first, final_norm):
    s = pl.program_id(0)
    j = lax.rem(jnp.maximum(s - n_first, 0), nf)

    @pl.when(s < n_first)
    def _():
        copy = pltpu.make_async_copy(y0_hbm.at[pl.ds(pl.multiple_of(s * FFN_TM, FFN_TM), FFN_TM)], o_ref, sem)
        copy.start()
        copy.wait()

    @pl.when(s >= n_first)
    def _():
        @pl.when(j == 0)
        def _():
            x = x_ref[...]
            h_ref[...] = _rms(x, g_ref[...]).astype(BF16)
            o_ref[...] = x

        _swiglu_step(h_ref, w1_ref[...], w3_ref[...], w2_ref[...], o_ref)

        if final_norm:
            @pl.when(j == nf - 1)
            def _():
                o_ref[...] = _rms(o_ref[...], fg_ref[...])


def _ffn(x, gain, w13, w2, layer, final_gain=None):
    m, d = x.shape
    f = w2.shape[1]
    final_norm = final_gain is not None
    if final_gain is None:
        final_gain = gain
    vec = pl.BlockSpec((1, d), lambda s: (0, 0))

    nf0 = f // FFN_TF_FIRST
    y0, w1b, w3b, w2b = pl.pallas_call(
        functools.partial(_ffn_first_kernel, final_norm=final_norm),
        out_shape=[jax.ShapeDtypeStruct((FFN_FIRST_ROWS, d), F32),
                   jax.ShapeDtypeStruct((d, f), BF16),
                   jax.ShapeDtypeStruct((d, f), BF16),
                   jax.ShapeDtypeStruct((f, d), BF16)],
        grid=(nf0,),
        in_specs=[
            _HBM,
            vec,
            pl.BlockSpec((None, d, FFN_TF_FIRST), lambda j: (layer, 0, j)),
            pl.BlockSpec((None, d, FFN_TF_FIRST), lambda j: (layer, 0, j + nf0)),
            pl.BlockSpec((None, FFN_TF_FIRST, d), lambda j: (layer, j, 0)),
            vec,
        ],
        out_specs=[
            pl.BlockSpec((FFN_FIRST_ROWS, d), lambda j: (0, 0)),
            pl.BlockSpec((d, FFN_TF_FIRST), lambda j: (0, j)),
            pl.BlockSpec((d, FFN_TF_FIRST), lambda j: (0, j)),
            pl.BlockSpec((FFN_TF_FIRST, d), lambda j: (j, 0)),
        ],
        scratch_shapes=[pltpu.VMEM((FFN_FIRST_ROWS, d), BF16), pltpu.SemaphoreType.DMA(())],
        compiler_params=_params(("arbitrary",)),
        name="ffn_first",
    )(x, gain, w13, w13, w2, final_gain)

    nf = f // FFN_TF
    n_first = FFN_FIRST_ROWS // FFN_TM
    n_rest = m // FFN_TM - n_first
    rest = lambda s: jnp.maximum(s - n_first, 0)
    out_row = lambda s: (jnp.where(s < n_first, s, n_first + rest(s) // nf), 0)
    x_row = lambda s: (n_first + rest(s) // nf, 0)
    w_col = lambda s: lax.rem(rest(s), nf)
    return pl.pallas_call(
        functools.partial(_ffn_rest_kernel, nf=nf, n_first=n_first, final_norm=final_norm),
        out_shape=jax.ShapeDtypeStruct((m, d), F32),
        grid=(n_first + n_rest * nf,),
        in_specs=[
            pl.BlockSpec((FFN_TM, d), x_row),
            vec,
            pl.BlockSpec((d, FFN_TF), lambda s: (0, w_col(s))),
            pl.BlockSpec((d, FFN_TF), lambda s: (0, w_col(s))),
            pl.BlockSpec((FFN_TF, d), lambda s: (w_col(s), 0)),
            vec,
            _HBM,
        ],
        out_specs=pl.BlockSpec((FFN_TM, d), out_row),
        scratch_shapes=[pltpu.VMEM((FFN_TM, d), BF16), pltpu.SemaphoreType.DMA(())],
        compiler_params=_params(("arbitrary",)),
        name="ffn_rest",
    )(x, gain, w1b, w3b, w2b, final_gain, y0)


def _gmlp_kernel(x_ref, g_ref, win_hbm, lng_ref, lnb_ref, ws_ref, bs_ref, wout_hbm, o_ref,
                 win_ref, win_stage, win_sem, wout_ref, wout_stage, wout_sem, u_ref, v_ref, uf_ref,
                 *, layer):
    e = GMLP_WIDTH

    @pl.when(pl.program_id(0) == 0)
    def _():
        _load_weight(win_hbm.at[layer], win_ref, win_stage, win_sem)
        _load_weight(wout_hbm.at[layer], wout_ref, wout_stage, wout_sem)

    x = x_ref[...]
    h = _rms(x, g_ref[...]).astype(BF16)

    def gelu(z):
        return 0.5 * z * (1.0 + lax.erf(z * (2.0 ** -0.5)))

    v = gelu(_dot(h, win_ref[:, e:]))
    u_ref[...] = gelu(_dot(h, win_ref[:, :e]))
    mu = jnp.mean(v, axis=-1, keepdims=True)
    vc = v - mu
    var = jnp.mean(vc * vc, axis=-1, keepdims=True)
    v_ref[...] = (vc * lax.rsqrt(var + LN_EPS) * lng_ref[...] + lnb_ref[...]).astype(BF16)

    t_idx = lax.broadcasted_iota(jnp.int32, (CHUNK, CHUNK), 0)
    s_idx = lax.broadcasted_iota(jnp.int32, (CHUNK, CHUNK), 1)
    causal = t_idx >= s_idx
    for g in range(GMLP_GROUPS):
        w = jnp.where(causal, ws_ref[g], 0.0).astype(BF16)
        cols = slice(g * GMLP_GROUP_DIM, (g + 1) * GMLP_GROUP_DIM)
        for c in range(MIX_TM // CHUNK):
            rows = slice(c * CHUNK, (c + 1) * CHUNK)
            f = _dot(w, v_ref[rows, cols]) + bs_ref[g]
            uf_ref[rows, cols] = (u_ref[rows, cols] * f).astype(BF16)
    o_ref[...] = x + _dot(uf_ref[...], wout_ref[...])


def _gmlp(x, gain, w_in, ln_g, ln_b, w_s, b_s, w_out, layer):
    m, d = x.shape
    e = GMLP_WIDTH
    row = lambda i: (i, 0)
    bias = jnp.broadcast_to(b_s[:, :, None], (GMLP_GROUPS, CHUNK, GMLP_GROUP_DIM))
    return pl.pallas_call(
        functools.partial(_gmlp_kernel, layer=layer),
        out_shape=jax.ShapeDtypeStruct((m, d), F32),
        grid=(m // MIX_TM,),
        in_specs=[
            pl.BlockSpec((MIX_TM, d), row),
            _resident((1, d)),
            _HBM,
            _resident((1, e)),
            _resident((1, e)),
            _resident((GMLP_GROUPS, CHUNK, CHUNK)),
            _resident((GMLP_GROUPS, CHUNK, GMLP_GROUP_DIM)),
            _HBM,
        ],
        out_specs=pl.BlockSpec((MIX_TM, d), row),
        scratch_shapes=[*_stage_scratch((d, 2 * e)),
                        *_stage_scratch((e, d)),
                        pltpu.VMEM((MIX_TM, e), F32),
                        pltpu.VMEM((MIX_TM, e), BF16),
                        pltpu.VMEM((MIX_TM, e), BF16)],
        compiler_params=_params(("arbitrary",)),
        name="gmlp",
    )(x, gain, w_in, ln_g, ln_b, w_s, bias, w_out)


def _conv_kernel(x_ref, g_ref, win_hbm, cw_ref, wout_hbm, o_ref,
                 win_ref, win_stage, win_sem, wout_ref, wout_stage, wout_sem, z_ref, gb_ref, *, layer):
    d = D_MODEL
    tm = MIX_TM
    i = pl.program_id(0)

    @pl.when(i == 0)
    def _():
        _load_weight(win_hbm.at[layer], win_ref, win_stage, win_sem)
        _load_weight(wout_hbm.at[layer], wout_ref, wout_stage, wout_sem)
        z_ref[0:SUBLANES, :] = jnp.zeros((SUBLANES, d), F32)

    @pl.when(i > 0)
    def _():
        z_ref[0:SUBLANES, :] = z_ref[tm:tm + SUBLANES, :]

    x = x_ref[...]
    h = _rms(x, g_ref[...]).astype(BF16)
    for jb in range(d // CONV_TN):
        lo = jb * CONV_TN
        cols = slice(lo, lo + CONV_TN)
        gate_b = _dot(h, win_ref[:, lo:lo + CONV_TN])
        gate_c = _dot(h, win_ref[:, d + lo:d + lo + CONV_TN])
        val = _dot(h, win_ref[:, 2 * d + lo:2 * d + lo + CONV_TN])
        z = gate_c * val
        z_ref[SUBLANES:SUBLANES + tm, cols] = z
        conv = (cw_ref[0:1, cols] * z_ref[SUBLANES - 2:SUBLANES - 2 + tm, cols]
                + cw_ref[1:2, cols] * z_ref[SUBLANES - 1:SUBLANES - 1 + tm, cols]
                + cw_ref[2:3, cols] * z)
        gb_ref[:, cols] = (gate_b * conv).astype(BF16)
    o_ref[...] = x + _dot(gb_ref[...], wout_ref[...])


def _short_conv(x, gain, w_in, conv_w, w_out, layer):
    m, d = x.shape
    row = lambda i: (i, 0)
    return pl.pallas_call(
        functools.partial(_conv_kernel, layer=layer),
        out_shape=jax.ShapeDtypeStruct((m, d), F32),
        grid=(m // MIX_TM,),
        in_specs=[
            pl.BlockSpec((MIX_TM, d), row),
            _resident((1, d)),
            _HBM,
            _resident((CONV_WIDTH, d)),
            _HBM,
        ],
        out_specs=pl.BlockSpec((MIX_TM, d), row),
        scratch_shapes=[*_stage_scratch((d, 3 * d)),
                        *_stage_scratch((d, d)),
                        pltpu.VMEM((MIX_TM + SUBLANES, d), F32),
                        pltpu.VMEM((MIX_TM, d), BF16)],
        compiler_params=_params(("arbitrary",)),
        name="short_conv",
    )(x, gain, w_in, conv_w, w_out)


def _kv_kernel(mem_ref, g_ref, wkv_ref, kv_ref):
    mem_n = _rms(mem_ref[...], g_ref[...]).astype(BF16)
    kv_ref[...] = _dot(mem_n, wkv_ref[...].astype(BF16)).astype(BF16)


def _kv_proj(mem, gain, wkv, layer):
    mlen, d = mem.shape
    n = wkv.shape[2]
    return pl.pallas_call(
        _kv_kernel,
        out_shape=jax.ShapeDtypeStruct((mlen, n), BF16),
        grid=(n // KV_TN,),
        in_specs=[
            pl.BlockSpec((mlen, d), lambda j: (0, 0)),
            pl.BlockSpec((1, d), lambda j: (0, 0)),
            pl.BlockSpec((None, d, KV_TN), lambda j: (layer, 0, j)),
        ],
        out_specs=pl.BlockSpec((mlen, KV_TN), lambda j: (0, j)),
        compiler_params=_params(("parallel",)),
        name="kv_proj",
    )(mem, gain, wkv)


def _xattn_kernel(x_ref, xn_ref, g_ref, wq_hbm, k_ref, v_ref, wo_hbm, o_ref,
                  wq_ref, wo_ref, stage, sem, qa_ref, qb_ref, oa_ref, ob_ref, *, layer):
    hd = XATTN_HEAD_DIM
    tm = ATT_TM

    def normed(x):
        return _rms(x, g_ref[...]).astype(BF16)

    def attend(q_ref, x, ob_scr, h_next, q_next):
        for hh in range(XATTN_HEADS):
            cols = slice(hh * hd, (hh + 1) * hd)
            s = lax.dot_general(q_ref[:, cols], k_ref[:, cols], (((1,), (1,)), ((), ())),
                                preferred_element_type=F32) * (hd ** -0.5)
            q_next[:, cols] = _dot(h_next, wq_ref[:, cols]).astype(BF16)
            p = jnp.exp(s - jnp.max(s, axis=-1, keepdims=True))
            p = p / jnp.sum(p, axis=-1, keepdims=True)
            ob_scr[:, cols] = _dot(p.astype(BF16), v_ref[:, cols]).astype(BF16)
        return x + _dot(ob_scr[...], wo_ref[...])

    @pl.when(pl.program_id(0) == 0)
    def _():
        _load_weight(wq_hbm.at[layer], wq_ref, stage, sem)
        _load_weight(wo_hbm.at[layer], wo_ref, stage, sem)
        qa_ref[...] = _dot(normed(x_ref[0:tm, :]), wq_ref[...]).astype(BF16)

    x0 = x_ref[0:tm, :]
    x1 = x_ref[tm:2 * tm, :]
    o_ref[0:tm, :] = attend(qa_ref, x0, oa_ref, normed(x1), qb_ref)
    o_ref[tm:2 * tm, :] = attend(qb_ref, x1, ob_ref, normed(xn_ref[...]), qa_ref)


def _xattn(x, gain, wq, kv, wo, layer):
    m, d = x.shape
    mlen = kv.shape[0]
    n = m // (2 * ATT_TM)
    row = lambda i: (i, 0)
    w_scratch, stage, sem = _stage_scratch((d, d))
    half = pltpu.VMEM((ATT_TM, d), BF16)
    return pl.pallas_call(
        functools.partial(_xattn_kernel, layer=layer),
        out_shape=jax.ShapeDtypeStruct((m, d), F32),
        grid=(n,),
        in_specs=[
            pl.BlockSpec((2 * ATT_TM, d), row),
            pl.BlockSpec((ATT_TM, d), lambda i: (jnp.minimum(2 * i + 2, 2 * n - 1), 0)),
            _resident((1, d)),
            _HBM,
            pl.BlockSpec((mlen, d), lambda i: (0, 0), pipeline_mode=pl.Buffered(1)),
            pl.BlockSpec((mlen, d), lambda i: (0, 1), pipeline_mode=pl.Buffered(1)),
            _HBM,
        ],
        out_specs=pl.BlockSpec((2 * ATT_TM, d), row),
        scratch_shapes=[w_scratch, w_scratch, stage, sem, half, half, half, half],
        compiler_params=_params(("arbitrary",)),
        name="xattn",
    )(x, x, gain, wq, kv, kv, wo)


def kernel(x, mem, ffn1_norm, ffn1_w13, ffn1_w2, mix_norm, gmlp_w_in, gmlp_ln_g, gmlp_ln_b, gmlp_w_s, gmlp_b_s, gmlp_w_out, conv_w_in, conv_w, conv_w_out, xattn_norm, mem_norm, xattn_wq, xattn_wkv, xattn_wo, ffn2_norm, ffn2_w13, ffn2_w2, final_norm):
    bsz, seq, d = x.shape
    assert (bsz, seq, d) == (1, SEQ, D_MODEL), "conv carry assumes one sequence laid out along rows"
    xs = x.reshape(seq, d)
    mem2 = mem.reshape(MEM_LEN, d)
    vec = lambda v: v.reshape(1, -1)

    for i in range(DEPTH):
        xs = _ffn(xs, vec(ffn1_norm[i]), ffn1_w13, ffn1_w2, i)
        j = i // 2
        if i % 2 == 0:
            xs = _gmlp(xs, vec(mix_norm[i]), gmlp_w_in, vec(gmlp_ln_g[j]), vec(gmlp_ln_b[j]),
                       gmlp_w_s[j], gmlp_b_s[j], gmlp_w_out, j)
        else:
            xs = _short_conv(xs, vec(mix_norm[i]), conv_w_in, conv_w[j], conv_w_out, j)
        kv = _kv_proj(mem2, vec(mem_norm[i]), xattn_wkv, i)
        xs = _xattn(xs, vec(xattn_norm[i]), xattn_wq, kv, xattn_wo, i)
        last = i == DEPTH - 1
        xs = _ffn(xs, vec(ffn2_norm[i]), ffn2_w13, ffn2_w2, i,
                  final_gain=vec(final_norm) if last else None)
    return xs.reshape(bsz, seq, d)
```

```python
import functools

import jax
import jax.numpy as jnp
from jax import lax
from jax.experimental import pallas as pl
from jax.experimental.pallas import tpu as pltpu

D_MODEL = 2048
SEQ = 8192
DEPTH = 2
MEM_LEN = 256
D_FF = 5632
CHUNK = 128
GMLP_WIDTH = 2048
GMLP_GROUPS = 8
GMLP_GROUP_DIM = GMLP_WIDTH // GMLP_GROUPS
CONV_WIDTH = 3
XATTN_HEADS = 4
XATTN_HEAD_DIM = D_MODEL // XATTN_HEADS
RMS_EPS = 1e-6
LN_EPS = 1e-5

F32 = jnp.float32
BF16 = jnp.bfloat16

V7X_VMEM_LIMIT_BYTES = 60 * 1024 * 1024
SUBLANES = 8

FFN_TM = 1024
FFN_TF = 512
FFN_TF_FIRST = 512
FFN_FIRST_ROWS = FFN_TM
NORM_ROWS = 256
MIX_TM = 256
ATT_TM = 256
CONV_TN = 512
KV_TN = 512
STAGE_BYTES = 2 * 1024 * 1024


def _rms(x, g):
    return x * lax.rsqrt(jnp.mean(x * x, axis=-1, keepdims=True) + RMS_EPS) * g


def _dot(a, b):
    return jnp.dot(a, b, preferred_element_type=F32)


def _params(semantics):
    return pltpu.CompilerParams(dimension_semantics=semantics,
                                vmem_limit_bytes=V7X_VMEM_LIMIT_BYTES)


def _resident(shape):
    return pl.BlockSpec(shape, lambda *_: (0,) * len(shape), pipeline_mode=pl.Buffered(1))


_HBM = pl.BlockSpec(memory_space=pl.ANY)


def _stage_rows(w_shape):
    rows = STAGE_BYTES // (w_shape[1] * 4) // 16 * 16
    while w_shape[0] % rows:
        rows -= 16
    return rows


def _stage_scratch(w_shape):
    return [pltpu.VMEM(w_shape, BF16),
            pltpu.VMEM((2, _stage_rows(w_shape), w_shape[1]), F32),
            pltpu.SemaphoreType.DMA((2,))]


def _load_weight(w_hbm, w_vmem, stage, sem):
    rows = stage.shape[1]
    n = w_hbm.shape[0] // rows

    def copy(c, slot):
        return pltpu.make_async_copy(w_hbm.at[pl.ds(c * rows, rows)], stage.at[slot], sem.at[slot])

    copy(0, 0).start()

    def body(c, carry):
        slot = lax.rem(c, 2)

        @pl.when(c + 1 < n)
        def _():
            copy(c + 1, 1 - slot).start()

        copy(c, slot).wait()
        w_vmem[pl.ds(pl.multiple_of(c * rows, rows), rows), :] = stage[slot].astype(BF16)
        return carry

    lax.fori_loop(0, n, body, 0)


def _swiglu_step(h_ref, w1, w3, w2, o_ref):
    h = h_ref[...]
    gate = _dot(h, w1)
    up = _dot(h, w3)
    act = (gate * (0.5 / (1.0 + jnp.exp(-gate))) * up).astype(BF16)
    o_ref[...] += _dot(act, w2)


def _ffn_first_kernel(x_hbm, g_ref, w1_ref, w3_ref, w2_ref, fg_ref, o_ref, w1b_ref, w3b_ref, w2b_ref,
                      h_ref, sem, *, final_norm):
    j = pl.program_id(0)

    @pl.when(j == 0)
    def _():
        copy = pltpu.make_async_copy(x_hbm.at[pl.ds(0, FFN_FIRST_ROWS)], o_ref, sem)
        copy.start()
        copy.wait()
        for r in range(0, FFN_FIRST_ROWS, NORM_ROWS):
            h_ref[r:r + NORM_ROWS, :] = _rms(o_ref[r:r + NORM_ROWS, :], g_ref[...]).astype(BF16)

    w1 = w1_ref[...].astype(BF16)
    w3 = w3_ref[...].astype(BF16)
    w2 = w2_ref[...].astype(BF16)
    w1b_ref[...] = w1
    w3b_ref[...] = w3
    w2b_ref[...] = w2
    _swiglu_step(h_ref, w1, w3, w2, o_ref)

    if final_norm:
        @pl.when(j == pl.num_programs(0) - 1)
        def _():
            for r in range(0, FFN_FIRST_ROWS, NORM_ROWS):
                o_ref[r:r + NORM_ROWS, :] = _rms(o_ref[r:r + NORM_ROWS, :], fg_ref[...])


def _ffn_rest_kernel(x_ref, g_ref, w1_ref, w3_ref, w2_ref, fg_ref, y0_hbm, o_ref, h_ref, sem,
                     *, nf, n_first, final_norm):
    s = pl.program_id(0)
    j = lax.rem(jnp.maximum(s - n_first, 0), nf)

    @pl.when(s < n_first)
    def _():
        copy = pltpu.make_async_copy(y0_hbm.at[pl.ds(pl.multiple_of(s * FFN_TM, FFN_TM), FFN_TM)], o_ref, sem)
        copy.start()
        copy.wait()

    @pl.when(s >= n_first)
    def _():
        @pl.when(j == 0)
        def _():
            x = x_ref[...]
            h_ref[...] = _rms(x, g_ref[...]).astype(BF16)
            o_ref[...] = x

        _swiglu_step(h_ref, w1_ref[...], w3_ref[...], w2_ref[...], o_ref)

        if final_norm:
            @pl.when(j == nf - 1)
            def _():
                o_ref[...] = _rms(o_ref[...], fg_ref[...])


def _ffn(x, gain, w13, w2, layer, final_gain=None):
    m, d = x.shape
    f = w2.shape[1]
    final_norm = final_gain is not None
    if final_gain is None:
        final_gain = gain
    vec = pl.BlockSpec((1, d), lambda s: (0, 0))

    nf0 = f // FFN_TF_FIRST
    y0, w1b, w3b, w2b = pl.pallas_call(
        functools.partial(_ffn_first_kernel, final_norm=final_norm),
        out_shape=[jax.ShapeDtypeStruct((FFN_FIRST_ROWS, d), F32),
                   jax.ShapeDtypeStruct((d, f), BF16),
                   jax.ShapeDtypeStruct((d, f), BF16),
                   jax.ShapeDtypeStruct((f, d), BF16)],
        grid=(nf0,),
        in_specs=[
            _HBM,
            vec,
            pl.BlockSpec((None, d, FFN_TF_FIRST), lambda j: (layer, 0, j)),
            pl.BlockSpec((None, d, FFN_TF_FIRST), lambda j: (layer, 0, j + nf0)),
            pl.BlockSpec((None, FFN_TF_FIRST, d), lambda j: (layer, j, 0)),
            vec,
        ],
        out_specs=[
            pl.BlockSpec((FFN_FIRST_ROWS, d), lambda j: (0, 0)),
            pl.BlockSpec((d, FFN_TF_FIRST), lambda j: (0, j)),
            pl.BlockSpec((d, FFN_TF_FIRST), lambda j: (0, j)),
            pl.BlockSpec((FFN_TF_FIRST, d), lambda j: (j, 0)),
        ],
        scratch_shapes=[pltpu.VMEM((FFN_FIRST_ROWS, d), BF16), pltpu.SemaphoreType.DMA(())],
        compiler_params=_params(("arbitrary",)),
        name="ffn_first",
    )(x, gain, w13, w13, w2, final_gain)

    nf = f // FFN_TF
    n_first = FFN_FIRST_ROWS // FFN_TM
    n_rest = m // FFN_TM - n_first
    rest = lambda s: jnp.maximum(s - n_first, 0)
    out_row = lambda s: (jnp.where(s < n_first, s, n_first + rest(s) // nf), 0)
    x_row = lambda s: (n_first + rest(s) // nf, 0)
    w_col = lambda s: lax.rem(rest(s), nf)
    return pl.pallas_call(
        functools.partial(_ffn_rest_kernel, nf=nf, n_first=n_first, final_norm=final_norm),
        out_shape=jax.ShapeDtypeStruct((m, d), F32),
        grid=(n_first + n_rest * nf,),
        in_specs=[
            pl.BlockSpec((FFN_TM, d), x_row),
            vec,
            pl.BlockSpec((d, FFN_TF), lambda s: (0, w_col(s))),
            pl.BlockSpec((d, FFN_TF), lambda s: (0, w_col(s))),
            pl.BlockSpec((FFN_TF, d), lambda s: (w_col(s), 0)),
            vec,
            _HBM,
        ],
        out_specs=pl.BlockSpec((FFN_TM, d), out_row),
        scratch_shapes=[pltpu.VMEM((FFN_TM, d), BF16), pltpu.SemaphoreType.DMA(())],
        compiler_params=_params(("arbitrary",)),
        name="ffn_rest",
    )(x, gain, w1b, w3b, w2b, final_gain, y0)


def _gmlp_kernel(x_ref, g_ref, win_hbm, lng_ref, lnb_ref, ws_ref, bs_ref, wout_hbm, o_ref,
                 win_ref, win_stage, win_sem, wout_ref, wout_stage, wout_sem, u_ref, v_ref, uf_ref,
                 *, layer):
    e = GMLP_WIDTH

    @pl.when(pl.program_id(0) == 0)
    def _():
        _load_weight(win_hbm.at[layer], win_ref, win_stage, win_sem)
        _load_weight(wout_hbm.at[layer], wout_ref, wout_stage, wout_sem)

    x = x_ref[...]
    h = _rms(x, g_ref[...]).astype(BF16)

    def gelu(z):
        return 0.5 * z * (1.0 + lax.erf(z * (2.0 ** -0.5)))

    v = gelu(_dot(h, win_ref[:, e:]))
    u_ref[...] = gelu(_dot(h, win_ref[:, :e]))
    mu = jnp.mean(v, axis=-1, keepdims=True)
    vc = v - mu
    var = jnp.mean(vc * vc, axis=-1, keepdims=True)
    v_ref[...] = (vc * lax.rsqrt(var + LN_EPS) * lng_ref[...] + lnb_ref[...]).astype(BF16)

    t_idx = lax.broadcasted_iota(jnp.int32, (CHUNK, CHUNK), 0)
    s_idx = lax.broadcasted_iota(jnp.int32, (CHUNK, CHUNK), 1)
    causal = t_idx >= s_idx
    for g in range(GMLP_GROUPS):
        w = jnp.where(causal, ws_ref[g], 0.0).astype(BF16)
        cols = slice(g * GMLP_GROUP_DIM, (g + 1) * GMLP_GROUP_DIM)
        for c in range(MIX_TM // CHUNK):
            rows = slice(c * CHUNK, (c + 1) * CHUNK)
            f = _dot(w, v_ref[rows, cols]) + bs_ref[g]
            uf_ref[rows, cols] = (u_ref[rows, cols] * f).astype(BF16)
    o_ref[...] = x + _dot(uf_ref[...], wout_ref[...])


def _gmlp(x, gain, w_in, ln_g, ln_b, w_s, b_s, w_out, layer):
    m, d = x.shape
    e = GMLP_WIDTH
    row = lambda i: (i, 0)
    bias = jnp.broadcast_to(b_s[:, :, None], (GMLP_GROUPS, CHUNK, GMLP_GROUP_DIM))
    return pl.pallas_call(
        functools.partial(_gmlp_kernel, layer=layer),
        out_shape=jax.ShapeDtypeStruct((m, d), F32),
        grid=(m // MIX_TM,),
        in_specs=[
            pl.BlockSpec((MIX_TM, d), row),
            _resident((1, d)),
            _HBM,
            _resident((1, e)),
            _resident((1, e)),
            _resident((GMLP_GROUPS, CHUNK, CHUNK)),
            _resident((GMLP_GROUPS, CHUNK, GMLP_GROUP_DIM)),
            _HBM,
        ],
        out_specs=pl.BlockSpec((MIX_TM, d), row),
        scratch_shapes=[*_stage_scratch((d, 2 * e)),
                        *_stage_scratch((e, d)),
                        pltpu.VMEM((MIX_TM, e), F32),
                        pltpu.VMEM((MIX_TM, e), BF16),
                        pltpu.VMEM((MIX_TM, e), BF16)],
        compiler_params=_params(("arbitrary",)),
        name="gmlp",
    )(x, gain, w_in, ln_g, ln_b, w_s, bias, w_out)


def _conv_kernel(x_ref, g_ref, win_hbm, cw_ref, wout_hbm, o_ref,
                 win_ref, win_stage, win_sem, wout_ref, wout_stage, wout_sem, z_ref, gb_ref, *, layer):
    d = D_MODEL
    tm = MIX_TM
    i = pl.program_id(0)

    @pl.when(i == 0)
    def _():
        _load_weight(win_hbm.at[layer], win_ref, win_stage, win_sem)
        _load_weight(wout_hbm.at[layer], wout_ref, wout_stage, wout_sem)
        z_ref[0:SUBLANES, :] = jnp.zeros((SUBLANES, d), F32)

    @pl.when(i > 0)
    def _():
        z_ref[0:SUBLANES, :] = z_ref[tm:tm + SUBLANES, :]

    x = x_ref[...]
    h = _rms(x, g_ref[...]).astype(BF16)
    for jb in range(d // CONV_TN):
        lo = jb * CONV_TN
        cols = slice(lo, lo + CONV_TN)
        gate_b = _dot(h, win_ref[:, lo:lo + CONV_TN])
        gate_c = _dot(h, win_ref[:, d + lo:d + lo + CONV_TN])
        val = _dot(h, win_ref[:, 2 * d + lo:2 * d + lo + CONV_TN])
        z = gate_c * val
        z_ref[SUBLANES:SUBLANES + tm, cols] = z
        conv = (cw_ref[0:1, cols] * z_ref[SUBLANES - 2:SUBLANES - 2 + tm, cols]
                + cw_ref[1:2, cols] * z_ref[SUBLANES - 1:SUBLANES - 1 + tm, cols]
                + cw_ref[2:3, cols] * z)
        gb_ref[:, cols] = (gate_b * conv).astype(BF16)
    o_ref[...] = x + _dot(gb_ref[...], wout_ref[...])


def _short_conv(x, gain, w_in, conv_w, w_out, layer):
    m, d = x.shape
    row = lambda i: (i, 0)
    return pl.pallas_call(
        functools.partial(_conv_kernel, layer=layer),
        out_shape=jax.ShapeDtypeStruct((m, d), F32),
        grid=(m // MIX_TM,),
        in_specs=[
            pl.BlockSpec((MIX_TM, d), row),
            _resident((1, d)),
            _HBM,
            _resident((CONV_WIDTH, d)),
            _HBM,
        ],
        out_specs=pl.BlockSpec((MIX_TM, d), row),
        scratch_shapes=[*_stage_scratch((d, 3 * d)),
                        *_stage_scratch((d, d)),
                        pltpu.VMEM((MIX_TM + SUBLANES, d), F32),
                        pltpu.VMEM((MIX_TM, d), BF16)],
        compiler_params=_params(("arbitrary",)),
        name="short_conv",
    )(x, gain, w_in, conv_w, w_out)


def _kv_kernel(mem_ref, g_ref, wkv_ref, kv_ref):
    mem_n = _rms(mem_ref[...], g_ref[...]).astype(BF16)
    kv_ref[...] = _dot(mem_n, wkv_ref[...].astype(BF16)).astype(BF16)


def _kv_proj(mem, gain, wkv, layer):
    mlen, d = mem.shape
    n = wkv.shape[2]
    return pl.pallas_call(
        _kv_kernel,
        out_shape=jax.ShapeDtypeStruct((mlen, n), BF16),
        grid=(n // KV_TN,),
        in_specs=[
            pl.BlockSpec((mlen, d), lambda j: (0, 0)),
            pl.BlockSpec((1, d), lambda j: (0, 0)),
            pl.BlockSpec((None, d, KV_TN), lambda j: (layer, 0, j)),
        ],
        out_specs=pl.BlockSpec((mlen, KV_TN), lambda j: (0, j)),
        compiler_params=_params(("parallel",)),
        name="kv_proj",
    )(mem, gain, wkv)


def _xattn_kernel(x_ref, xn_ref, g_ref, wq_hbm, k_ref, v_ref, wo_hbm, o_ref,
                  wq_ref, wo_ref, stage, sem, qa_ref, qb_ref, oa_ref, ob_ref, *, layer):
    hd = XATTN_HEAD_DIM
    tm = ATT_TM

    def normed(x):
        return _rms(x, g_ref[...]).astype(BF16)

    def attend(q_ref, x, ob_scr, h_next, q_next):
        for hh in range(XATTN_HEADS):
            cols = slice(hh * hd, (hh + 1) * hd)
            s = lax.dot_general(q_ref[:, cols], k_ref[:, cols], (((1,), (1,)), ((), ())),
                                preferred_element_type=F32) * (hd ** -0.5)
            q_next[:, cols] = _dot(h_next, wq_ref[:, cols]).astype(BF16)
            p = jnp.exp(s - jnp.max(s, axis=-1, keepdims=True))
            p = p / jnp.sum(p, axis=-1, keepdims=True)
            ob_scr[:, cols] = _dot(p.astype(BF16), v_ref[:, cols]).astype(BF16)
        return x + _dot(ob_scr[...], wo_ref[...])

    @pl.when(pl.program_id(0) == 0)
    def _():
        _load_weight(wq_hbm.at[layer], wq_ref, stage, sem)
        _load_weight(wo_hbm.at[layer], wo_ref, stage, sem)
        qa_ref[...] = _dot(normed(x_ref[0:tm, :]), wq_ref[...]).astype(BF16)

    x0 = x_ref[0:tm, :]
    x1 = x_ref[tm:2 * tm, :]
    o_ref[0:tm, :] = attend(qa_ref, x0, oa_ref, normed(x1), qb_ref)
    o_ref[tm:2 * tm, :] = attend(qb_ref, x1, ob_ref, normed(xn_ref[...]), qa_ref)


def _xattn(x, gain, wq, kv, wo, layer):
    m, d = x.shape
    mlen = kv.shape[0]
    n = m // (2 * ATT_TM)
    row = lambda i: (i, 0)
    w_scratch, stage, sem = _stage_scratch((d, d))
    half = pltpu.VMEM((ATT_TM, d), BF16)
    return pl.pallas_call(
        functools.partial(_xattn_kernel, layer=layer),
        out_shape=jax.ShapeDtypeStruct((m, d), F32),
        grid=(n,),
        in_specs=[
            pl.BlockSpec((2 * ATT_TM, d), row),
            pl.BlockSpec((ATT_TM, d), lambda i: (jnp.minimum(2 * i + 2, 2 * n - 1), 0)),
            _resident((1, d)),
            _HBM,
            pl.BlockSpec((mlen, d), lambda i: (0, 0), pipeline_mode=pl.Buffered(1)),
            pl.BlockSpec((mlen, d), lambda i: (0, 1), pipeline_mode=pl.Buffered(1)),
            _HBM,
        ],
        out_specs=pl.BlockSpec((2 * ATT_TM, d), row),
        scratch_shapes=[w_scratch, w_scratch, stage, sem, half, half, half, half],
        compiler_params=_params(("arbitrary",)),
        name="xattn",
    )(x, x, gain, wq, kv, kv, wo)


def kernel(x, mem, ffn1_norm, ffn1_w13, ffn1_w2, mix_norm, gmlp_w_in, gmlp_ln_g, gmlp_ln_b, gmlp_w_s, gmlp_b_s, gmlp_w_out, conv_w_in, conv_w, conv_w_out, xattn_norm, mem_norm, xattn_wq, xattn_wkv, xattn_wo, ffn2_norm, ffn2_w13, ffn2_w2, final_norm):
    bsz, seq, d = x.shape
    assert (bsz, seq, d) == (1, SEQ, D_MODEL), "conv carry assumes one sequence laid out along rows"
    xs = x.reshape(seq, d)
    mem2 = mem.reshape(MEM_LEN, d)
    vec = lambda v: v.reshape(1, -1)

    for i in range(DEPTH):
        xs = _ffn(xs, vec(ffn1_norm[i]), ffn1_w13, ffn1_w2, i)
        j = i // 2
        if i % 2 == 0:
            xs = _gmlp(xs, vec(mix_norm[i]), gmlp_w_in, vec(gmlp_ln_g[j]), vec(gmlp_ln_b[j]),
                       gmlp_w_s[j], gmlp_b_s[j], gmlp_w_out, j)
        else:
            xs = _short_conv(xs, vec(mix_norm[i]), conv_w_in, conv_w[j], conv_w_out, j)
        kv = _kv_proj(mem2, vec(mem_norm[i]), xattn_wkv, i)
        xs = _xattn(xs, vec(xattn_norm[i]), xattn_wq, kv, xattn_wo, i)
        last = i == DEPTH - 1
        xs = _ffn(xs, vec(ffn2_norm[i]), ffn2_w13, ffn2_w2, i,
                  final_gain=vec(final_norm) if last else None)
    return xs.reshape(bsz, seq, d)
```

```python
import functools

import jax
import jax.numpy as jnp
from jax import lax
from jax.experimental import pallas as pl
from jax.experimental.pallas import tpu as pltpu

D_MODEL = 2048
SEQ = 8192
DEPTH = 2
MEM_LEN = 256
D_FF = 5632
CHUNK = 128
GMLP_WIDTH = 2048
GMLP_GROUPS = 8
GMLP_GROUP_DIM = GMLP_WIDTH // GMLP_GROUPS
CONV_WIDTH = 3
XATTN_HEADS = 4
XATTN_HEAD_DIM = D_MODEL // XATTN_HEADS
RMS_EPS = 1e-6
LN_EPS = 1e-5

F32 = jnp.float32
BF16 = jnp.bfloat16

V7X_VMEM_LIMIT_BYTES = 60 * 1024 * 1024
SUBLANES = 8

FFN_TM = 1024
FFN_TF = 512
FFN_TF_FIRST = 512
FFN_FIRST_ROWS = FFN_TM
NORM_ROWS = 256
SIDE_CAST_ROWS = 128
MIX_TM = 256
ATT_TM = 256
CONV_TN = 512
KV_TK = 256
STAGE_BYTES = 2 * 1024 * 1024


def _rms(x, g):
    return x * lax.rsqrt(jnp.mean(x * x, axis=-1, keepdims=True) + RMS_EPS) * g


def _dot(a, b):
    return jnp.dot(a, b, preferred_element_type=F32)


def _params(semantics):
    return pltpu.CompilerParams(dimension_semantics=semantics,
                                vmem_limit_bytes=V7X_VMEM_LIMIT_BYTES)


def _resident(shape):
    return pl.BlockSpec(shape, lambda *_: (0,) * len(shape), pipeline_mode=pl.Buffered(1))


_HBM = pl.BlockSpec(memory_space=pl.ANY)


def _stage_rows(w_shape):
    rows = STAGE_BYTES // (w_shape[1] * 4) // 16 * 16
    while w_shape[0] % rows:
        rows -= 16
    return rows


def _stage_scratch(w_shape):
    return [pltpu.VMEM(w_shape, BF16),
            pltpu.VMEM((2, _stage_rows(w_shape), w_shape[1]), F32),
            pltpu.SemaphoreType.DMA((2,))]


def _load_weight(w_hbm, w_vmem, stage, sem):
    rows = stage.shape[1]
    n = w_hbm.shape[0] // rows

    def copy(c, slot):
        return pltpu.make_async_copy(w_hbm.at[pl.ds(c * rows, rows)], stage.at[slot], sem.at[slot])

    copy(0, 0).start()

    def body(c, carry):
        slot = lax.rem(c, 2)

        @pl.when(c + 1 < n)
        def _():
            copy(c + 1, 1 - slot).start()

        copy(c, slot).wait()
        w_vmem[pl.ds(pl.multiple_of(c * rows, rows), rows), :] = stage[slot].astype(BF16)
        return carry

    lax.fori_loop(0, n, body, 0)


def _swiglu_step(h_ref, w1, w3, w2, o_ref):
    h = h_ref[...]
    gate = _dot(h, w1)
    up = _dot(h, w3)
    act = (gate * (0.5 / (1.0 + jnp.exp(-gate))) * up).astype(BF16)
    o_ref[...] += _dot(act, w2)


def _ffn_first_kernel(x_hbm, g_ref, w1_ref, w3_ref, w2_ref, fg_ref, o_ref, w1b_ref, w3b_ref, *rest,
                      final_norm, cast_w2):
    if cast_w2:
        w2b_ref, h_ref, sem = rest
    else:
        h_ref, sem = rest
    j = pl.program_id(0)

    @pl.when(j == 0)
    def _():
        copy = pltpu.make_async_copy(x_hbm.at[pl.ds(0, FFN_FIRST_ROWS)], o_ref, sem)
        copy.start()
        copy.wait()
        for r in range(0, FFN_FIRST_ROWS, NORM_ROWS):
            h_ref[r:r + NORM_ROWS, :] = _rms(o_ref[r:r + NORM_ROWS, :], g_ref[...]).astype(BF16)

    w1 = w1_ref[...].astype(BF16)
    w3 = w3_ref[...].astype(BF16)
    w2 = w2_ref[...].astype(BF16)
    w1b_ref[...] = w1
    w3b_ref[...] = w3
    if cast_w2:
        w2b_ref[...] = w2
    _swiglu_step(h_ref, w1, w3, w2, o_ref)

    if final_norm:
        @pl.when(j == pl.num_programs(0) - 1)
        def _():
            for r in range(0, FFN_FIRST_ROWS, NORM_ROWS):
                o_ref[r:r + NORM_ROWS, :] = _rms(o_ref[r:r + NORM_ROWS, :], fg_ref[...])


def _ffn_rest_kernel(x_ref, g_ref, w1_ref, w3_ref, w2_ref, fg_ref, y0_hbm, *rest,
                     nf, n_first, final_norm, cast_next):
    if cast_next:
        nw2_ref, o_ref, nw2b_ref, h_ref, sem = rest
    else:
        o_ref, h_ref, sem = rest
    s = pl.program_id(0)
    j = lax.rem(jnp.maximum(s - n_first, 0), nf)

    @pl.when(s < n_first)
    def _():
        copy = pltpu.make_async_copy(y0_hbm.at[pl.ds(pl.multiple_of(s * FFN_TM, FFN_TM), FFN_TM)], o_ref, sem)
        copy.start()
        copy.wait()

    @pl.when(s >= n_first)
    def _():
        @pl.when(j == 0)
        def _():
            x = x_ref[...]
            h_ref[...] = _rms(x, g_ref[...]).astype(BF16)
            o_ref[...] = x

        if cast_next:
            nw2b_ref[...] = nw2_ref[...].astype(BF16)
        _swiglu_step(h_ref, w1_ref[...], w3_ref[...], w2_ref[...], o_ref)

        if final_norm:
            @pl.when(j == nf - 1)
            def _():
                o_ref[...] = _rms(o_ref[...], fg_ref[...])


def _ffn(x, gain, w13, w2, layer, final_gain=None, w2_bf16=None, next_w2=None):
    m, d = x.shape
    f = w2.shape[1]
    final_norm = final_gain is not None
    if final_gain is None:
        final_gain = gain
    vec = pl.BlockSpec((1, d), lambda s: (0, 0))
    cast_w2 = w2_bf16 is None

    nf0 = f // FFN_TF_FIRST
    w_tile = pl.BlockSpec((d, FFN_TF_FIRST), lambda j: (0, j))
    w2_tile = pl.BlockSpec((FFN_TF_FIRST, d), lambda j: (j, 0))
    w_copy = jax.ShapeDtypeStruct((d, f), BF16)
    first = pl.pallas_call(
        functools.partial(_ffn_first_kernel, final_norm=final_norm, cast_w2=cast_w2),
        out_shape=[jax.ShapeDtypeStruct((FFN_FIRST_ROWS, d), F32), w_copy, w_copy]
        + ([jax.ShapeDtypeStruct((f, d), BF16)] if cast_w2 else []),
        grid=(nf0,),
        in_specs=[
            _HBM,
            vec,
            pl.BlockSpec((None, d, FFN_TF_FIRST), lambda j: (layer, 0, j)),
            pl.BlockSpec((None, d, FFN_TF_FIRST), lambda j: (layer, 0, j + nf0)),
            pl.BlockSpec((None, FFN_TF_FIRST, d), lambda j: (layer, j, 0)) if cast_w2 else w2_tile,
            vec,
        ],
        out_specs=[pl.BlockSpec((FFN_FIRST_ROWS, d), lambda j: (0, 0)), w_tile, w_tile]
        + ([w2_tile] if cast_w2 else []),
        scratch_shapes=[pltpu.VMEM((FFN_FIRST_ROWS, d), BF16), pltpu.SemaphoreType.DMA(())],
        compiler_params=_params(("arbitrary",)),
        name="ffn_first",
    )(x, gain, w13, w13, w2 if cast_w2 else w2_bf16, final_gain)
    if cast_w2:
        y0, w1b, w3b, w2b = first
    else:
        (y0, w1b, w3b), w2b = first, w2_bf16

    nf = f // FFN_TF
    n_first = FFN_FIRST_ROWS // FFN_TM
    n_rest = m // FFN_TM - n_first
    n_steps = n_first + n_rest * nf
    rest = lambda s: jnp.maximum(s - n_first, 0)
    out_row = lambda s: (jnp.where(s < n_first, s, n_first + rest(s) // nf), 0)
    x_row = lambda s: (n_first + rest(s) // nf, 0)
    w_col = lambda s: lax.rem(rest(s), nf)
    in_specs = [
        pl.BlockSpec((FFN_TM, d), x_row),
        vec,
        pl.BlockSpec((d, FFN_TF), lambda s: (0, w_col(s))),
        pl.BlockSpec((d, FFN_TF), lambda s: (0, w_col(s))),
        pl.BlockSpec((FFN_TF, d), lambda s: (w_col(s), 0)),
        vec,
        _HBM,
    ]
    out_shape = [jax.ShapeDtypeStruct((m, d), F32)]
    out_specs = [pl.BlockSpec((FFN_TM, d), out_row)]
    args = [x, gain, w1b, w3b, w2b, final_gain, y0]
    if next_w2 is not None:
        nw2, nlayer = next_w2
        n_tiles = f // SIDE_CAST_ROWS
        assert n_tiles <= n_steps
        assert n_tiles <= n_steps - n_first
        tile = lambda s: jnp.minimum(rest(s), n_tiles - 1)
        in_specs.append(pl.BlockSpec((None, SIDE_CAST_ROWS, d), lambda s: (nlayer, tile(s), 0)))
        out_shape.append(jax.ShapeDtypeStruct((f, d), BF16))
        out_specs.append(pl.BlockSpec((SIDE_CAST_ROWS, d), lambda s: (tile(s), 0)))
        args.append(nw2)
    res = pl.pallas_call(
        functools.partial(_ffn_rest_kernel, nf=nf, n_first=n_first, final_norm=final_norm,
                          cast_next=next_w2 is not None),
        out_shape=out_shape,
        grid=(n_steps,),
        in_specs=in_specs,
        out_specs=out_specs,
        scratch_shapes=[pltpu.VMEM((FFN_TM, d), BF16), pltpu.SemaphoreType.DMA(())],
        compiler_params=_params(("arbitrary",)),
        name="ffn_rest",
    )(*args)
    return (res[0], res[1]) if next_w2 is not None else (res[0], None)


def _gmlp_kernel(x_ref, g_ref, win_hbm, lng_ref, lnb_ref, ws_ref, bs_ref, wout_hbm, o_ref,
                 win_ref, win_stage, win_sem, wout_ref, wout_stage, wout_sem, u_ref, v_ref, uf_ref,
                 *, layer):
    e = GMLP_WIDTH

    @pl.when(pl.program_id(0) == 0)
    def _():
        _load_weight(win_hbm.at[layer], win_ref, win_stage, win_sem)
        _load_weight(wout_hbm.at[layer], wout_ref, wout_stage, wout_sem)

    x = x_ref[...]
    h = _rms(x, g_ref[...]).astype(BF16)

    def gelu(z):
        return 0.5 * z * (1.0 + lax.erf(z * (2.0 ** -0.5)))

    v = gelu(_dot(h, win_ref[:, e:]))
    u_ref[...] = gelu(_dot(h, win_ref[:, :e]))
    mu = jnp.mean(v, axis=-1, keepdims=True)
    vc = v - mu
    var = jnp.mean(vc * vc, axis=-1, keepdims=True)
    v_ref[...] = (vc * lax.rsqrt(var + LN_EPS) * lng_ref[...] + lnb_ref[...]).astype(BF16)

    t_idx = lax.broadcasted_iota(jnp.int32, (CHUNK, CHUNK), 0)
    s_idx = lax.broadcasted_iota(jnp.int32, (CHUNK, CHUNK), 1)
    causal = t_idx >= s_idx
    for g in range(GMLP_GROUPS):
        w = jnp.where(causal, ws_ref[g], 0.0).astype(BF16)
        cols = slice(g * GMLP_GROUP_DIM, (g + 1) * GMLP_GROUP_DIM)
        for c in range(MIX_TM // CHUNK):
            rows = slice(c * CHUNK, (c + 1) * CHUNK)
            f = _dot(w, v_ref[rows, cols]) + bs_ref[g]
            uf_ref[rows, cols] = (u_ref[rows, cols] * f).astype(BF16)
    o_ref[...] = x + _dot(uf_ref[...], wout_ref[...])


def _gmlp(x, gain, w_in, ln_g, ln_b, w_s, b_s, w_out, layer):
    m, d = x.shape
    e = GMLP_WIDTH
    row = lambda i: (i, 0)
    bias = jnp.broadcast_to(b_s[:, :, None], (GMLP_GROUPS, CHUNK, GMLP_GROUP_DIM))
    return pl.pallas_call(
        functools.partial(_gmlp_kernel, layer=layer),
        out_shape=jax.ShapeDtypeStruct((m, d), F32),
        grid=(m // MIX_TM,),
        in_specs=[
            pl.BlockSpec((MIX_TM, d), row),
            _resident((1, d)),
            _HBM,
            _resident((1, e)),
            _resident((1, e)),
            _resident((GMLP_GROUPS, CHUNK, CHUNK)),
            _resident((GMLP_GROUPS, CHUNK, GMLP_GROUP_DIM)),
            _HBM,
        ],
        out_specs=pl.BlockSpec((MIX_TM, d), row),
        scratch_shapes=[*_stage_scratch((d, 2 * e)),
                        *_stage_scratch((e, d)),
                        pltpu.VMEM((MIX_TM, e), F32),
                        pltpu.VMEM((MIX_TM, e), BF16),
                        pltpu.VMEM((MIX_TM, e), BF16)],
        compiler_params=_params(("arbitrary",)),
        name="gmlp",
    )(x, gain, w_in, ln_g, ln_b, w_s, bias, w_out)


def _conv_kernel(x_ref, g_ref, win_hbm, cw_ref, wout_hbm, o_ref,
                 win_ref, win_stage, win_sem, wout_ref, wout_stage, wout_sem, z_ref, gb_ref, *, layer):
    d = D_MODEL
    tm = MIX_TM
    i = pl.program_id(0)

    @pl.when(i == 0)
    def _():
        _load_weight(win_hbm.at[layer], win_ref, win_stage, win_sem)
        _load_weight(wout_hbm.at[layer], wout_ref, wout_stage, wout_sem)
        z_ref[0:SUBLANES, :] = jnp.zeros((SUBLANES, d), F32)

    @pl.when(i > 0)
    def _():
        z_ref[0:SUBLANES, :] = z_ref[tm:tm + SUBLANES, :]

    x = x_ref[...]
    h = _rms(x, g_ref[...]).astype(BF16)
    for jb in range(d // CONV_TN):
        lo = jb * CONV_TN
        cols = slice(lo, lo + CONV_TN)
        gate_b = _dot(h, win_ref[:, lo:lo + CONV_TN])
        gate_c = _dot(h, win_ref[:, d + lo:d + lo + CONV_TN])
        val = _dot(h, win_ref[:, 2 * d + lo:2 * d + lo + CONV_TN])
        z = gate_c * val
        z_ref[SUBLANES:SUBLANES + tm, cols] = z
        conv = (cw_ref[0:1, cols] * z_ref[SUBLANES - 2:SUBLANES - 2 + tm, cols]
                + cw_ref[1:2, cols] * z_ref[SUBLANES - 1:SUBLANES - 1 + tm, cols]
                + cw_ref[2:3, cols] * z)
        gb_ref[:, cols] = (gate_b * conv).astype(BF16)
    o_ref[...] = x + _dot(gb_ref[...], wout_ref[...])


def _short_conv(x, gain, w_in, conv_w, w_out, layer):
    m, d = x.shape
    row = lambda i: (i, 0)
    return pl.pallas_call(
        functools.partial(_conv_kernel, layer=layer),
        out_shape=jax.ShapeDtypeStruct((m, d), F32),
        grid=(m // MIX_TM,),
        in_specs=[
            pl.BlockSpec((MIX_TM, d), row),
            _resident((1, d)),
            _HBM,
            _resident((CONV_WIDTH, d)),
            _HBM,
        ],
        out_specs=pl.BlockSpec((MIX_TM, d), row),
        scratch_shapes=[*_stage_scratch((d, 3 * d)),
                        *_stage_scratch((d, d)),
                        pltpu.VMEM((MIX_TM + SUBLANES, d), F32),
                        pltpu.VMEM((MIX_TM, d), BF16)],
        compiler_params=_params(("arbitrary",)),
        name="short_conv",
    )(x, gain, w_in, conv_w, w_out)


def _kv_kernel(mem_ref, memk_ref, gk_ref, wkv_ref, kv_ref, acc_ref, inv_ref):
    k = pl.program_id(0)

    @pl.when(k == 0)
    def _():
        mem = mem_ref[...]
        inv_ref[...] = lax.rsqrt(jnp.mean(mem * mem, axis=-1, keepdims=True) + RMS_EPS)
        acc_ref[...] = jnp.zeros_like(acc_ref)

    mem_n = (memk_ref[...] * inv_ref[...] * gk_ref[...]).astype(BF16)
    acc_ref[...] += _dot(mem_n, wkv_ref[...].astype(BF16))

    @pl.when(k == pl.num_programs(0) - 1)
    def _():
        kv_ref[...] = acc_ref[...].astype(BF16)


def _kv_proj(mem, gain, wkv, layer):
    mlen, d = mem.shape
    n = wkv.shape[2]
    return pl.pallas_call(
        _kv_kernel,
        out_shape=jax.ShapeDtypeStruct((mlen, n), BF16),
        grid=(d // KV_TK,),
        in_specs=[
            pl.BlockSpec((mlen, d), lambda k: (0, 0)),
            pl.BlockSpec((mlen, KV_TK), lambda k: (0, k)),
            pl.BlockSpec((1, KV_TK), lambda k: (0, k)),
            pl.BlockSpec((None, KV_TK, n), lambda k: (layer, k, 0)),
        ],
        out_specs=pl.BlockSpec((mlen, n), lambda k: (0, 0)),
        scratch_shapes=[pltpu.VMEM((mlen, n), F32), pltpu.VMEM((mlen, 1), F32)],
        compiler_params=_params(("arbitrary",)),
        name="kv_proj",
    )(mem, mem, gain, wkv)


def _xattn_kernel(x_ref, xn_ref, g_ref, wq_hbm, k_ref, v_ref, wo_hbm, nw2_ref, o_ref, nw2b_ref,
                  wq_ref, wo_ref, stage, sem, qa_ref, qb_ref, oa_ref, ob_ref, *, layer):
    hd = XATTN_HEAD_DIM
    tm = ATT_TM

    def normed(x):
        return _rms(x, g_ref[...]).astype(BF16)

    def attend(q_ref, x, ob_scr, h_next, q_next):
        for hh in range(XATTN_HEADS):
            cols = slice(hh * hd, (hh + 1) * hd)
            s = lax.dot_general(q_ref[:, cols], k_ref[:, cols], (((1,), (1,)), ((), ())),
                                preferred_element_type=F32) * (hd ** -0.5)
            q_next[:, cols] = _dot(h_next, wq_ref[:, cols]).astype(BF16)
            p = jnp.exp(s - jnp.max(s, axis=-1, keepdims=True))
            p = p / jnp.sum(p, axis=-1, keepdims=True)
            ob_scr[:, cols] = _dot(p.astype(BF16), v_ref[:, cols]).astype(BF16)
        return x + _dot(ob_scr[...], wo_ref[...])

    @pl.when(pl.program_id(0) == 0)
    def _():
        _load_weight(wq_hbm.at[layer], wq_ref, stage, sem)
        _load_weight(wo_hbm.at[layer], wo_ref, stage, sem)
        qa_ref[...] = _dot(normed(x_ref[0:tm, :]), wq_ref[...]).astype(BF16)

    nw2b_ref[...] = nw2_ref[...].astype(BF16)

    x0 = x_ref[0:tm, :]
    x1 = x_ref[tm:2 * tm, :]
    o_ref[0:tm, :] = attend(qa_ref, x0, oa_ref, normed(x1), qb_ref)
    o_ref[tm:2 * tm, :] = attend(qb_ref, x1, ob_ref, normed(xn_ref[...]), qa_ref)


def _xattn(x, gain, wq, kv, wo, layer, next_w2):
    m, d = x.shape
    mlen = kv.shape[0]
    f = next_w2.shape[1]
    n = m // (2 * ATT_TM)
    w2_rows = f // n
    assert f % n == 0 and w2_rows % 16 == 0
    row = lambda i: (i, 0)
    w_scratch, stage, sem = _stage_scratch((d, d))
    half = pltpu.VMEM((ATT_TM, d), BF16)
    return pl.pallas_call(
        functools.partial(_xattn_kernel, layer=layer),
        out_shape=[jax.ShapeDtypeStruct((m, d), F32), jax.ShapeDtypeStruct((f, d), BF16)],
        grid=(n,),
        in_specs=[
            pl.BlockSpec((2 * ATT_TM, d), row),
            pl.BlockSpec((ATT_TM, d), lambda i: (jnp.minimum(2 * i + 2, 2 * n - 1), 0)),
            _resident((1, d)),
            _HBM,
            pl.BlockSpec((mlen, d), lambda i: (0, 0), pipeline_mode=pl.Buffered(1)),
            pl.BlockSpec((mlen, d), lambda i: (0, 1), pipeline_mode=pl.Buffered(1)),
            _HBM,
            pl.BlockSpec((None, w2_rows, d), lambda i: (layer, i, 0)),
        ],
        out_specs=[pl.BlockSpec((2 * ATT_TM, d), row), pl.BlockSpec((w2_rows, d), row)],
        scratch_shapes=[w_scratch, w_scratch, stage, sem, half, half, half, half],
        compiler_params=_params(("arbitrary",)),
        name="xattn",
    )(x, x, gain, wq, kv, kv, wo, next_w2)


def kernel(x, mem, ffn1_norm, ffn1_w13, ffn1_w2, mix_norm, gmlp_w_in, gmlp_ln_g, gmlp_ln_b, gmlp_w_s, gmlp_b_s, gmlp_w_out, conv_w_in, conv_w, conv_w_out, xattn_norm, mem_norm, xattn_wq, xattn_wkv, xattn_wo, ffn2_norm, ffn2_w13, ffn2_w2, final_norm):
    bsz, seq, d = x.shape
    assert (bsz, seq, d) == (1, SEQ, D_MODEL), "conv carry assumes one sequence laid out along rows"
    xs = x.reshape(seq, d)
    mem2 = mem.reshape(MEM_LEN, d)
    vec = lambda v: v.reshape(1, -1)

    ffn1_w2_bf16 = None
    for i in range(DEPTH):
        xs, _ = _ffn(xs, vec(ffn1_norm[i]), ffn1_w13, ffn1_w2, i, w2_bf16=ffn1_w2_bf16)
        j = i // 2
        if i % 2 == 0:
            xs = _gmlp(xs, vec(mix_norm[i]), gmlp_w_in, vec(gmlp_ln_g[j]), vec(gmlp_ln_b[j]),
                       gmlp_w_s[j], gmlp_b_s[j], gmlp_w_out, j)
        else:
            xs = _short_conv(xs, vec(mix_norm[i]), conv_w_in, conv_w[j], conv_w_out, j)
        kv = _kv_proj(mem2, vec(mem_norm[i]), xattn_wkv, i)
        xs, ffn2_w2_bf16 = _xattn(xs, vec(xattn_norm[i]), xattn_wq, kv, xattn_wo, i, ffn2_w2)
        last = i == DEPTH - 1
        xs, ffn1_w2_bf16 = _ffn(xs, vec(ffn2_norm[i]), ffn2_w13, ffn2_w2, i,
                                final_gain=vec(final_norm) if last else None,
                                w2_bf16=ffn2_w2_bf16,
                                next_w2=None if last else (ffn1_w2, i + 1))
    return xs.reshape(bsz, seq, d)
```

```python
import functools

import jax
import jax.numpy as jnp
from jax import lax
from jax.experimental import pallas as pl
from jax.experimental.pallas import tpu as pltpu

D_MODEL = 2048
SEQ = 8192
DEPTH = 2
MEM_LEN = 256
D_FF = 5632
CHUNK = 128
GMLP_WIDTH = 2048
GMLP_GROUPS = 8
GMLP_GROUP_DIM = GMLP_WIDTH // GMLP_GROUPS
CONV_WIDTH = 3
XATTN_HEADS = 4
XATTN_HEAD_DIM = D_MODEL // XATTN_HEADS
RMS_EPS = 1e-6
LN_EPS = 1e-5

F32 = jnp.float32
BF16 = jnp.bfloat16

V7X_VMEM_LIMIT_BYTES = 60 * 1024 * 1024
SUBLANES = 8

FFN_TM = 1024
FFN_TF = 512
FFN_TF_FIRST = 512
FFN_FIRST_ROWS = FFN_TM
FFN_NORM_ROWS = 128
NORM_ROWS = 256
MIX_TM = 256
ATT_TM = 256
CONV_TN = 512
KV_TN = 512
STAGE_BYTES = 2 * 1024 * 1024


def _rms(x, g):
    return x * lax.rsqrt(jnp.mean(x * x, axis=-1, keepdims=True) + RMS_EPS) * g


def _dot(a, b):
    return jnp.dot(a, b, preferred_element_type=F32)


def _params(semantics):
    return pltpu.CompilerParams(dimension_semantics=semantics,
                                vmem_limit_bytes=V7X_VMEM_LIMIT_BYTES)


def _resident(shape):
    return pl.BlockSpec(shape, lambda *_: (0,) * len(shape), pipeline_mode=pl.Buffered(1))


_HBM = pl.BlockSpec(memory_space=pl.ANY)


def _stage_rows(w_shape):
    rows = STAGE_BYTES // (w_shape[1] * 4) // 16 * 16
    while w_shape[0] % rows:
        rows -= 16
    return rows


def _stage_scratch(w_shape):
    return [pltpu.VMEM(w_shape, BF16),
            pltpu.VMEM((2, _stage_rows(w_shape), w_shape[1]), F32),
            pltpu.SemaphoreType.DMA((2,))]


def _load_weight(w_hbm, w_vmem, stage, sem):
    rows = stage.shape[1]
    n = w_hbm.shape[0] // rows

    def copy(c, slot):
        return pltpu.make_async_copy(w_hbm.at[pl.ds(c * rows, rows)], stage.at[slot], sem.at[slot])

    copy(0, 0).start()

    def body(c, carry):
        slot = lax.rem(c, 2)

        @pl.when(c + 1 < n)
        def _():
            copy(c + 1, 1 - slot).start()

        copy(c, slot).wait()
        w_vmem[pl.ds(pl.multiple_of(c * rows, rows), rows), :] = stage[slot].astype(BF16)
        return carry

    lax.fori_loop(0, n, body, 0)


def _swiglu(h, w1, w3, w2):
    gate = _dot(h, w1)
    up = _dot(h, w3)
    act = (gate * (0.5 / (1.0 + jnp.exp(-gate))) * up).astype(BF16)
    return _dot(act, w2)


def _ffn_first_kernel(x_hbm, g_ref, w1_ref, w3_ref, w2_ref, fg_ref, o_ref, w1b_ref, w3b_ref, w2b_ref,
                      h_ref, sem, *, final_norm):
    j = pl.program_id(0)

    @pl.when(j == 0)
    def _():
        copy = pltpu.make_async_copy(x_hbm.at[pl.ds(0, FFN_FIRST_ROWS)], o_ref, sem)
        copy.start()
        copy.wait()
        for r in range(0, FFN_FIRST_ROWS, NORM_ROWS):
            h_ref[r:r + NORM_ROWS, :] = _rms(o_ref[r:r + NORM_ROWS, :], g_ref[...]).astype(BF16)

    w1 = w1_ref[...].astype(BF16)
    w3 = w3_ref[...].astype(BF16)
    w2 = w2_ref[...].astype(BF16)
    w1b_ref[...] = w1
    w3b_ref[...] = w3
    w2b_ref[...] = w2
    o_ref[...] += _swiglu(h_ref[...], w1, w3, w2)

    if final_norm:
        @pl.when(j == pl.num_programs(0) - 1)
        def _():
            for r in range(0, FFN_FIRST_ROWS, NORM_ROWS):
                o_ref[r:r + NORM_ROWS, :] = _rms(o_ref[r:r + NORM_ROWS, :], fg_ref[...])


def _ffn_rest_kernel(x_ref, xn_ref, g_ref, w1_ref, w3_ref, w2_ref, fg_ref, y0_hbm, o_ref, h_ref, sem,
                     *, nf, n_first, final_norm):
    s = pl.program_id(0)
    t = jnp.maximum(s - n_first, 0)
    j = lax.rem(t, nf)
    slot = lax.rem(t // nf, 2)

    @pl.when(s < n_first)
    def _():
        copy = pltpu.make_async_copy(y0_hbm.at[pl.ds(pl.multiple_of(s * FFN_TM, FFN_TM), FFN_TM)], o_ref, sem)
        copy.start()
        h_ref[0] = _rms(x_ref[...], g_ref[...]).astype(BF16)
        copy.wait()

    @pl.when(s >= n_first)
    def _():
        @pl.when(j == 0)
        def _():
            o_ref[...] = x_ref[...]

        o_ref[...] += _swiglu(h_ref[slot], w1_ref[...], w3_ref[...], w2_ref[...])
        chunk = jnp.minimum(j, FFN_TM // FFN_NORM_ROWS - 1)
        h_ref[1 - slot, pl.ds(pl.multiple_of(chunk * FFN_NORM_ROWS, FFN_NORM_ROWS), FFN_NORM_ROWS), :] = (
            _rms(xn_ref[...], g_ref[...]).astype(BF16))

        if final_norm:
            @pl.when(j == nf - 1)
            def _():
                o_ref[...] = _rms(o_ref[...], fg_ref[...])


def _ffn(x, gain, w13, w2, layer, final_gain=None):
    m, d = x.shape
    f = w2.shape[1]
    final_norm = final_gain is not None
    if final_gain is None:
        final_gain = gain
    vec = pl.BlockSpec((1, d), lambda s: (0, 0))

    nf0 = f // FFN_TF_FIRST
    y0, w1b, w3b, w2b = pl.pallas_call(
        functools.partial(_ffn_first_kernel, final_norm=final_norm),
        out_shape=[jax.ShapeDtypeStruct((FFN_FIRST_ROWS, d), F32),
                   jax.ShapeDtypeStruct((d, f), BF16),
                   jax.ShapeDtypeStruct((d, f), BF16),
                   jax.ShapeDtypeStruct((f, d), BF16)],
        grid=(nf0,),
        in_specs=[
            _HBM,
            vec,
            pl.BlockSpec((None, d, FFN_TF_FIRST), lambda j: (layer, 0, j)),
            pl.BlockSpec((None, d, FFN_TF_FIRST), lambda j: (layer, 0, j + nf0)),
            pl.BlockSpec((None, FFN_TF_FIRST, d), lambda j: (layer, j, 0)),
            vec,
        ],
        out_specs=[
            pl.BlockSpec((FFN_FIRST_ROWS, d), lambda j: (0, 0)),
            pl.BlockSpec((d, FFN_TF_FIRST), lambda j: (0, j)),
            pl.BlockSpec((d, FFN_TF_FIRST), lambda j: (0, j)),
            pl.BlockSpec((FFN_TF_FIRST, d), lambda j: (j, 0)),
        ],
        scratch_shapes=[pltpu.VMEM((FFN_FIRST_ROWS, d), BF16), pltpu.SemaphoreType.DMA(())],
        compiler_params=_params(("arbitrary",)),
        name="ffn_first",
    )(x, gain, w13, w13, w2, final_gain)

    nf = f // FFN_TF
    n_first = FFN_FIRST_ROWS // FFN_TM
    n_rest = m // FFN_TM - n_first
    chunks = FFN_TM // FFN_NORM_ROWS
    assert n_first == 1 and chunks <= nf
    rest = lambda s: jnp.maximum(s - n_first, 0)
    blk = lambda s: n_first + rest(s) // nf
    out_row = lambda s: (jnp.where(s < n_first, s, blk(s)), 0)
    w_col = lambda s: lax.rem(rest(s), nf)
    next_chunk = lambda s: (jnp.minimum((blk(s) + 1) * chunks + jnp.minimum(w_col(s), chunks - 1),
                                        m // FFN_NORM_ROWS - 1), 0)
    return pl.pallas_call(
        functools.partial(_ffn_rest_kernel, nf=nf, n_first=n_first, final_norm=final_norm),
        out_shape=jax.ShapeDtypeStruct((m, d), F32),
        grid=(n_first + n_rest * nf,),
        in_specs=[
            pl.BlockSpec((FFN_TM, d), lambda s: (blk(s), 0)),
            pl.BlockSpec((FFN_NORM_ROWS, d), next_chunk),
            vec,
            pl.BlockSpec((d, FFN_TF), lambda s: (0, w_col(s))),
            pl.BlockSpec((d, FFN_TF), lambda s: (0, w_col(s))),
            pl.BlockSpec((FFN_TF, d), lambda s: (w_col(s), 0)),
            vec,
            _HBM,
        ],
        out_specs=pl.BlockSpec((FFN_TM, d), out_row),
        scratch_shapes=[pltpu.VMEM((2, FFN_TM, d), BF16), pltpu.SemaphoreType.DMA(())],
        compiler_params=_params(("arbitrary",)),
        name="ffn_rest",
    )(x, x, gain, w1b, w3b, w2b, final_gain, y0)


def _gmlp_kernel(x_ref, g_ref, win_hbm, lng_ref, lnb_ref, ws_ref, bs_ref, wout_hbm, o_ref,
                 win_ref, win_stage, win_sem, wout_ref, wout_stage, wout_sem, u_ref, v_ref, uf_ref,
                 *, layer):
    e = GMLP_WIDTH

    @pl.when(pl.program_id(0) == 0)
    def _():
        _load_weight(win_hbm.at[layer], win_ref, win_stage, win_sem)
        _load_weight(wout_hbm.at[layer], wout_ref, wout_stage, wout_sem)

    x = x_ref[...]
    h = _rms(x, g_ref[...]).astype(BF16)

    def gelu(z):
        return 0.5 * z * (1.0 + lax.erf(z * (2.0 ** -0.5)))

    v = gelu(_dot(h, win_ref[:, e:]))
    u_ref[...] = gelu(_dot(h, win_ref[:, :e]))
    mu = jnp.mean(v, axis=-1, keepdims=True)
    vc = v - mu
    var = jnp.mean(vc * vc, axis=-1, keepdims=True)
    v_ref[...] = (vc * lax.rsqrt(var + LN_EPS) * lng_ref[...] + lnb_ref[...]).astype(BF16)

    t_idx = lax.broadcasted_iota(jnp.int32, (CHUNK, CHUNK), 0)
    s_idx = lax.broadcasted_iota(jnp.int32, (CHUNK, CHUNK), 1)
    causal = t_idx >= s_idx
    for g in range(GMLP_GROUPS):
        w = jnp.where(causal, ws_ref[g], 0.0).astype(BF16)
        cols = slice(g * GMLP_GROUP_DIM, (g + 1) * GMLP_GROUP_DIM)
        for c in range(MIX_TM // CHUNK):
            rows = slice(c * CHUNK, (c + 1) * CHUNK)
            f = _dot(w, v_ref[rows, cols]) + bs_ref[g]
            uf_ref[rows, cols] = (u_ref[rows, cols] * f).astype(BF16)
    o_ref[...] = x + _dot(uf_ref[...], wout_ref[...])


def _gmlp(x, gain, w_in, ln_g, ln_b, w_s, b_s, w_out, layer):
    m, d = x.shape
    e = GMLP_WIDTH
    row = lambda i: (i, 0)
    bias = jnp.broadcast_to(b_s[:, :, None], (GMLP_GROUPS, CHUNK, GMLP_GROUP_DIM))
    return pl.pallas_call(
        functools.partial(_gmlp_kernel, layer=layer),
        out_shape=jax.ShapeDtypeStruct((m, d), F32),
        grid=(m // MIX_TM,),
        in_specs=[
            pl.BlockSpec((MIX_TM, d), row),
            _resident((1, d)),
            _HBM,
            _resident((1, e)),
            _resident((1, e)),
            _resident((GMLP_GROUPS, CHUNK, CHUNK)),
            _resident((GMLP_GROUPS, CHUNK, GMLP_GROUP_DIM)),
            _HBM,
        ],
        out_specs=pl.BlockSpec((MIX_TM, d), row),
        scratch_shapes=[*_stage_scratch((d, 2 * e)),
                        *_stage_scratch((e, d)),
                        pltpu.VMEM((MIX_TM, e), F32),
                        pltpu.VMEM((MIX_TM, e), BF16),
                        pltpu.VMEM((MIX_TM, e), BF16)],
        compiler_params=_params(("arbitrary",)),
        name="gmlp",
    )(x, gain, w_in, ln_g, ln_b, w_s, bias, w_out)


def _conv_kernel(x_ref, g_ref, win_hbm, cw_ref, wout_hbm, o_ref,
                 win_ref, win_stage, win_sem, wout_ref, wout_stage, wout_sem, z_ref, gb_ref, *, layer):
    d = D_MODEL
    tm = MIX_TM
    i = pl.program_id(0)

    @pl.when(i == 0)
    def _():
        _load_weight(win_hbm.at[layer], win_ref, win_stage, win_sem)
        _load_weight(wout_hbm.at[layer], wout_ref, wout_stage, wout_sem)
        z_ref[0:SUBLANES, :] = jnp.zeros((SUBLANES, d), F32)

    @pl.when(i > 0)
    def _():
        z_ref[0:SUBLANES, :] = z_ref[tm:tm + SUBLANES, :]

    x = x_ref[...]
    h = _rms(x, g_ref[...]).astype(BF16)
    for jb in range(d // CONV_TN):
        lo = jb * CONV_TN
        cols = slice(lo, lo + CONV_TN)
        gate_b = _dot(h, win_ref[:, lo:lo + CONV_TN])
        gate_c = _dot(h, win_ref[:, d + lo:d + lo + CONV_TN])
        val = _dot(h, win_ref[:, 2 * d + lo:2 * d + lo + CONV_TN])
        z = gate_c * val
        z_ref[SUBLANES:SUBLANES + tm, cols] = z
        conv = (cw_ref[0:1, cols] * z_ref[SUBLANES - 2:SUBLANES - 2 + tm, cols]
                + cw_ref[1:2, cols] * z_ref[SUBLANES - 1:SUBLANES - 1 + tm, cols]
                + cw_ref[2:3, cols] * z)
        gb_ref[:, cols] = (gate_b * conv).astype(BF16)
    o_ref[...] = x + _dot(gb_ref[...], wout_ref[...])


def _short_conv(x, gain, w_in, conv_w, w_out, layer):
    m, d = x.shape
    row = lambda i: (i, 0)
    return pl.pallas_call(
        functools.partial(_conv_kernel, layer=layer),
        out_shape=jax.ShapeDtypeStruct((m, d), F32),
        grid=(m // MIX_TM,),
        in_specs=[
            pl.BlockSpec((MIX_TM, d), row),
            _resident((1, d)),
            _HBM,
            _resident((CONV_WIDTH, d)),
            _HBM,
        ],
        out_specs=pl.BlockSpec((MIX_TM, d), row),
        scratch_shapes=[*_stage_scratch((d, 3 * d)),
                        *_stage_scratch((d, d)),
                        pltpu.VMEM((MIX_TM + SUBLANES, d), F32),
                        pltpu.VMEM((MIX_TM, d), BF16)],
        compiler_params=_params(("arbitrary",)),
        name="short_conv",
    )(x, gain, w_in, conv_w, w_out)


def _kv_kernel(mem_ref, g_ref, wkv_ref, kv_ref):
    mem_n = _rms(mem_ref[...], g_ref[...]).astype(BF16)
    kv_ref[...] = _dot(mem_n, wkv_ref[...].astype(BF16)).astype(BF16)


def _kv_proj(mem, gain, wkv, layer):
    mlen, d = mem.shape
    n = wkv.shape[2]
    return pl.pallas_call(
        _kv_kernel,
        out_shape=jax.ShapeDtypeStruct((mlen, n), BF16),
        grid=(n // KV_TN,),
        in_specs=[
            pl.BlockSpec((mlen, d), lambda j: (0, 0)),
            pl.BlockSpec((1, d), lambda j: (0, 0)),
            pl.BlockSpec((None, d, KV_TN), lambda j: (layer, 0, j)),
        ],
        out_specs=pl.BlockSpec((mlen, KV_TN), lambda j: (0, j)),
        compiler_params=_params(("parallel",)),
        name="kv_proj",
    )(mem, gain, wkv)


def _xattn_kernel(x_ref, xn_ref, g_ref, wq_hbm, k_ref, v_ref, wo_hbm, o_ref,
                  wq_ref, wo_ref, stage, sem, qa_ref, qb_ref, oa_ref, ob_ref, *, layer):
    hd = XATTN_HEAD_DIM
    tm = ATT_TM

    def normed(x):
        return _rms(x, g_ref[...]).astype(BF16)

    def attend(q_ref, x, ob_scr, h_next, q_next):
        for hh in range(XATTN_HEADS):
            cols = slice(hh * hd, (hh + 1) * hd)
            s = lax.dot_general(q_ref[:, cols], k_ref[:, cols], (((1,), (1,)), ((), ())),
                                preferred_element_type=F32) * (hd ** -0.5)
            q_next[:, cols] = _dot(h_next, wq_ref[:, cols]).astype(BF16)
            p = jnp.exp(s - jnp.max(s, axis=-1, keepdims=True))
            p = p / jnp.sum(p, axis=-1, keepdims=True)
            ob_scr[:, cols] = _dot(p.astype(BF16), v_ref[:, cols]).astype(BF16)
        return x + _dot(ob_scr[...], wo_ref[...])

    @pl.when(pl.program_id(0) == 0)
    def _():
        _load_weight(wq_hbm.at[layer], wq_ref, stage, sem)
        _load_weight(wo_hbm.at[layer], wo_ref, stage, sem)
        qa_ref[...] = _dot(normed(x_ref[0:tm, :]), wq_ref[...]).astype(BF16)

    x0 = x_ref[0:tm, :]
    x1 = x_ref[tm:2 * tm, :]
    o_ref[0:tm, :] = attend(qa_ref, x0, oa_ref, normed(x1), qb_ref)
    o_ref[tm:2 * tm, :] = attend(qb_ref, x1, ob_ref, normed(xn_ref[...]), qa_ref)


def _xattn(x, gain, wq, kv, wo, layer):
    m, d = x.shape
    mlen = kv.shape[0]
    n = m // (2 * ATT_TM)
    row = lambda i: (i, 0)
    w_scratch, stage, sem = _stage_scratch((d, d))
    half = pltpu.VMEM((ATT_TM, d), BF16)
    return pl.pallas_call(
        functools.partial(_xattn_kernel, layer=layer),
        out_shape=jax.ShapeDtypeStruct((m, d), F32),
        grid=(n,),
        in_specs=[
            pl.BlockSpec((2 * ATT_TM, d), row),
            pl.BlockSpec((ATT_TM, d), lambda i: (jnp.minimum(2 * i + 2, 2 * n - 1), 0)),
            _resident((1, d)),
            _HBM,
            pl.BlockSpec((mlen, d), lambda i: (0, 0), pipeline_mode=pl.Buffered(1)),
            pl.BlockSpec((mlen, d), lambda i: (0, 1), pipeline_mode=pl.Buffered(1)),
            _HBM,
        ],
        out_specs=pl.BlockSpec((2 * ATT_TM, d), row),
        scratch_shapes=[w_scratch, w_scratch, stage, sem, half, half, half, half],
        compiler_params=_params(("arbitrary",)),
        name="xattn",
    )(x, x, gain, wq, kv, kv, wo)


def kernel(x, mem, ffn1_norm, ffn1_w13, ffn1_w2, mix_norm, gmlp_w_in, gmlp_ln_g, gmlp_ln_b, gmlp_w_s, gmlp_b_s, gmlp_w_out, conv_w_in, conv_w, conv_w_out, xattn_norm, mem_norm, xattn_wq, xattn_wkv, xattn_wo, ffn2_norm, ffn2_w13, ffn2_w2, final_norm):
    bsz, seq, d = x.shape
    assert (bsz, seq, d) == (1, SEQ, D_MODEL), "conv carry assumes one sequence laid out along rows"
    xs = x.reshape(seq, d)
    mem2 = mem.reshape(MEM_LEN, d)
    vec = lambda v: v.reshape(1, -1)

    for i in range(DEPTH):
        xs = _ffn(xs, vec(ffn1_norm[i]), ffn1_w13, ffn1_w2, i)
        j = i // 2
        if i % 2 == 0:
            xs = _gmlp(xs, vec(mix_norm[i]), gmlp_w_in, vec(gmlp_ln_g[j]), vec(gmlp_ln_b[j]),
                       gmlp_w_s[j], gmlp_b_s[j], gmlp_w_out, j)
        else:
            xs = _short_conv(xs, vec(mix_norm[i]), conv_w_in, conv_w[j], conv_w_out, j)
        kv = _kv_proj(mem2, vec(mem_norm[i]), xattn_wkv, i)
        xs = _xattn(xs, vec(xattn_norm[i]), xattn_wq, kv, xattn_wo, i)
        last = i == DEPTH - 1
        xs = _ffn(xs, vec(ffn2_norm[i]), ffn2_w13, ffn2_w2, i,
                  final_gain=vec(final_norm) if last else None)
    return xs.reshape(bsz, seq, d)
```

```python
import functools

import jax
import jax.numpy as jnp
from jax import lax
from jax.experimental import pallas as pl
from jax.experimental.pallas import tpu as pltpu

D_MODEL = 2048
SEQ = 8192
DEPTH = 2
MEM_LEN = 256
D_FF = 5632
CHUNK = 128
GMLP_WIDTH = 2048
GMLP_GROUPS = 8
GMLP_GROUP_DIM = GMLP_WIDTH // GMLP_GROUPS
CONV_WIDTH = 3
XATTN_HEADS = 4
XATTN_HEAD_DIM = D_MODEL // XATTN_HEADS
RMS_EPS = 1e-6
LN_EPS = 1e-5

F32 = jnp.float32
BF16 = jnp.bfloat16

V7X_VMEM_LIMIT_BYTES = 60 * 1024 * 1024
SUBLANES = 8

FFN_TM = 1024
FFN_TF = 512
FFN_FIRST_ROWS = FFN_TM
NORM_ROWS = 256
MIX_TM = 256
ATT_TM = 256
CONV_TN = 512
KV_TN = 512
STAGE_BYTES = 2 * 1024 * 1024


def _rms(x, g):
    return x * lax.rsqrt(jnp.mean(x * x, axis=-1, keepdims=True) + RMS_EPS) * g


def _dot(a, b):
    return jnp.dot(a, b, preferred_element_type=F32)


def _params(semantics):
    return pltpu.CompilerParams(dimension_semantics=semantics,
                                vmem_limit_bytes=V7X_VMEM_LIMIT_BYTES)


def _resident(shape):
    return pl.BlockSpec(shape, lambda *_: (0,) * len(shape), pipeline_mode=pl.Buffered(1))


_HBM = pl.BlockSpec(memory_space=pl.ANY)


def _stage_rows(w_shape):
    rows = STAGE_BYTES // (w_shape[1] * 4) // 16 * 16
    while w_shape[0] % rows:
        rows -= 16
    return rows


def _stage_scratch(w_shape):
    return [pltpu.VMEM(w_shape, BF16),
            pltpu.VMEM((2, _stage_rows(w_shape), w_shape[1]), F32),
            pltpu.SemaphoreType.DMA((2,))]


def _load_weight(w_hbm, w_vmem, stage, sem):
    rows = stage.shape[1]
    n = w_hbm.shape[0] // rows

    def copy(c, slot):
        return pltpu.make_async_copy(w_hbm.at[pl.ds(c * rows, rows)], stage.at[slot], sem.at[slot])

    copy(0, 0).start()

    def body(c, carry):
        slot = lax.rem(c, 2)

        @pl.when(c + 1 < n)
        def _():
            copy(c + 1, 1 - slot).start()

        copy(c, slot).wait()
        w_vmem[pl.ds(pl.multiple_of(c * rows, rows), rows), :] = stage[slot].astype(BF16)
        return carry

    lax.fori_loop(0, n, body, 0)


def _swiglu(h, w13, w2):
    tf = w13.shape[1] // 2
    gu = _dot(h, w13)
    gate = gu[:, :tf]
    up = gu[:, tf:]
    act = (gate * (0.5 / (1.0 + jnp.exp(-gate))) * up).astype(BF16)
    return _dot(act, w2)


def _ffn_first_kernel(x_hbm, g_ref, w1_ref, w3_ref, w2_ref, fg_ref, o_ref, w13b_ref, w2b_ref,
                      h_ref, sem, *, final_norm):
    j = pl.program_id(0)

    @pl.when(j == 0)
    def _():
        copy = pltpu.make_async_copy(x_hbm.at[pl.ds(0, FFN_FIRST_ROWS)], o_ref, sem)
        copy.start()
        copy.wait()
        for r in range(0, FFN_FIRST_ROWS, NORM_ROWS):
            h_ref[r:r + NORM_ROWS, :] = _rms(o_ref[r:r + NORM_ROWS, :], g_ref[...]).astype(BF16)

    w13 = jnp.concatenate([w1_ref[...].astype(BF16), w3_ref[...].astype(BF16)], axis=1)
    w2 = w2_ref[...].astype(BF16)
    w13b_ref[...] = w13
    w2b_ref[...] = w2
    o_ref[...] += _swiglu(h_ref[...], w13, w2)

    if final_norm:
        @pl.when(j == pl.num_programs(0) - 1)
        def _():
            for r in range(0, FFN_FIRST_ROWS, NORM_ROWS):
                o_ref[r:r + NORM_ROWS, :] = _rms(o_ref[r:r + NORM_ROWS, :], fg_ref[...])


def _ffn_rest_kernel(x_ref, g_ref, w13_ref, w2_ref, fg_ref, y0_hbm, o_ref, h_ref, sem,
                     *, nf, n_first, final_norm):
    s = pl.program_id(0)
    j = lax.rem(jnp.maximum(s - n_first, 0), nf)

    @pl.when(s < n_first)
    def _():
        copy = pltpu.make_async_copy(y0_hbm.at[pl.ds(pl.multiple_of(s * FFN_TM, FFN_TM), FFN_TM)], o_ref, sem)
        copy.start()
        copy.wait()

    @pl.when(s >= n_first)
    def _():
        @pl.when(j == 0)
        def _():
            x = x_ref[...]
            h_ref[...] = _rms(x, g_ref[...]).astype(BF16)
            o_ref[...] = x

        o_ref[...] += _swiglu(h_ref[...], w13_ref[...], w2_ref[...])

        if final_norm:
            @pl.when(j == nf - 1)
            def _():
                o_ref[...] = _rms(o_ref[...], fg_ref[...])


def _ffn(x, gain, w13, w2, layer, final_gain=None):
    m, d = x.shape
    f = w2.shape[1]
    final_norm = final_gain is not None
    if final_gain is None:
        final_gain = gain
    vec = pl.BlockSpec((1, d), lambda s: (0, 0))

    nf = f // FFN_TF
    y0, w13b, w2b = pl.pallas_call(
        functools.partial(_ffn_first_kernel, final_norm=final_norm),
        out_shape=[jax.ShapeDtypeStruct((FFN_FIRST_ROWS, d), F32),
                   jax.ShapeDtypeStruct((nf, d, 2 * FFN_TF), BF16),
                   jax.ShapeDtypeStruct((f, d), BF16)],
        grid=(nf,),
        in_specs=[
            _HBM,
            vec,
            pl.BlockSpec((None, d, FFN_TF), lambda j: (layer, 0, j)),
            pl.BlockSpec((None, d, FFN_TF), lambda j: (layer, 0, j + nf)),
            pl.BlockSpec((None, FFN_TF, d), lambda j: (layer, j, 0)),
            vec,
        ],
        out_specs=[
            pl.BlockSpec((FFN_FIRST_ROWS, d), lambda j: (0, 0)),
            pl.BlockSpec((None, d, 2 * FFN_TF), lambda j: (j, 0, 0)),
            pl.BlockSpec((FFN_TF, d), lambda j: (j, 0)),
        ],
        scratch_shapes=[pltpu.VMEM((FFN_FIRST_ROWS, d), BF16), pltpu.SemaphoreType.DMA(())],
        compiler_params=_params(("arbitrary",)),
        name="ffn_first",
    )(x, gain, w13, w13, w2, final_gain)

    n_first = FFN_FIRST_ROWS // FFN_TM
    n_rest = m // FFN_TM - n_first
    rest = lambda s: jnp.maximum(s - n_first, 0)
    out_row = lambda s: (jnp.where(s < n_first, s, n_first + rest(s) // nf), 0)
    x_row = lambda s: (n_first + rest(s) // nf, 0)
    w_col = lambda s: lax.rem(rest(s), nf)
    return pl.pallas_call(
        functools.partial(_ffn_rest_kernel, nf=nf, n_first=n_first, final_norm=final_norm),
        out_shape=jax.ShapeDtypeStruct((m, d), F32),
        grid=(n_first + n_rest * nf,),
        in_specs=[
            pl.BlockSpec((FFN_TM, d), x_row),
            vec,
            pl.BlockSpec((None, d, 2 * FFN_TF), lambda s: (w_col(s), 0, 0)),
            pl.BlockSpec((FFN_TF, d), lambda s: (w_col(s), 0)),
            vec,
            _HBM,
        ],
        out_specs=pl.BlockSpec((FFN_TM, d), out_row),
        scratch_shapes=[pltpu.VMEM((FFN_TM, d), BF16), pltpu.SemaphoreType.DMA(())],
        compiler_params=_params(("arbitrary",)),
        name="ffn_rest",
    )(x, gain, w13b, w2b, final_gain, y0)


def _gmlp_kernel(x_ref, g_ref, win_hbm, lng_ref, lnb_ref, ws_ref, bs_ref, wout_hbm, o_ref,
                 win_ref, win_stage, win_sem, wout_ref, wout_stage, wout_sem, u_ref, v_ref, uf_ref,
                 *, layer):
    e = GMLP_WIDTH

    @pl.when(pl.program_id(0) == 0)
    def _():
        _load_weight(win_hbm.at[layer], win_ref, win_stage, win_sem)
        _load_weight(wout_hbm.at[layer], wout_ref, wout_stage, wout_sem)

    x = x_ref[...]
    h = _rms(x, g_ref[...]).astype(BF16)

    def gelu(z):
        return 0.5 * z * (1.0 + lax.erf(z * (2.0 ** -0.5)))

    v = gelu(_dot(h, win_ref[:, e:]))
    u_ref[...] = gelu(_dot(h, win_ref[:, :e]))
    mu = jnp.mean(v, axis=-1, keepdims=True)
    vc = v - mu
    var = jnp.mean(vc * vc, axis=-1, keepdims=True)
    v_ref[...] = (vc * lax.rsqrt(var + LN_EPS) * lng_ref[...] + lnb_ref[...]).astype(BF16)

    t_idx = lax.broadcasted_iota(jnp.int32, (CHUNK, CHUNK), 0)
    s_idx = lax.broadcasted_iota(jnp.int32, (CHUNK, CHUNK), 1)
    causal = t_idx >= s_idx
    for g in range(GMLP_GROUPS):
        w = jnp.where(causal, ws_ref[g], 0.0).astype(BF16)
        cols = slice(g * GMLP_GROUP_DIM, (g + 1) * GMLP_GROUP_DIM)
        for c in range(MIX_TM // CHUNK):
            rows = slice(c * CHUNK, (c + 1) * CHUNK)
            f = _dot(w, v_ref[rows, cols]) + bs_ref[g]
            uf_ref[rows, cols] = (u_ref[rows, cols] * f).astype(BF16)
    o_ref[...] = x + _dot(uf_ref[...], wout_ref[...])


def _gmlp(x, gain, w_in, ln_g, ln_b, w_s, b_s, w_out, layer):
    m, d = x.shape
    e = GMLP_WIDTH
    row = lambda i: (i, 0)
    bias = jnp.broadcast_to(b_s[:, :, None], (GMLP_GROUPS, CHUNK, GMLP_GROUP_DIM))
    return pl.pallas_call(
        functools.partial(_gmlp_kernel, layer=layer),
        out_shape=jax.ShapeDtypeStruct((m, d), F32),
        grid=(m // MIX_TM,),
        in_specs=[
            pl.BlockSpec((MIX_TM, d), row),
            _resident((1, d)),
            _HBM,
            _resident((1, e)),
            _resident((1, e)),
            _resident((GMLP_GROUPS, CHUNK, CHUNK)),
            _resident((GMLP_GROUPS, CHUNK, GMLP_GROUP_DIM)),
            _HBM,
        ],
        out_specs=pl.BlockSpec((MIX_TM, d), row),
        scratch_shapes=[*_stage_scratch((d, 2 * e)),
                        *_stage_scratch((e, d)),
                        pltpu.VMEM((MIX_TM, e), F32),
                        pltpu.VMEM((MIX_TM, e), BF16),
                        pltpu.VMEM((MIX_TM, e), BF16)],
        compiler_params=_params(("arbitrary",)),
        name="gmlp",
    )(x, gain, w_in, ln_g, ln_b, w_s, bias, w_out)


def _conv_kernel(x_ref, g_ref, win_hbm, cw_ref, wout_hbm, o_ref,
                 win_ref, win_stage, win_sem, wout_ref, wout_stage, wout_sem, z_ref, gb_ref, *, layer):
    d = D_MODEL
    tm = MIX_TM
    i = pl.program_id(0)

    @pl.when(i == 0)
    def _():
        _load_weight(win_hbm.at[layer], win_ref, win_stage, win_sem)
        _load_weight(wout_hbm.at[layer], wout_ref, wout_stage, wout_sem)
        z_ref[0:SUBLANES, :] = jnp.zeros((SUBLANES, d), F32)

    @pl.when(i > 0)
    def _():
        z_ref[0:SUBLANES, :] = z_ref[tm:tm + SUBLANES, :]

    x = x_ref[...]
    h = _rms(x, g_ref[...]).astype(BF16)
    for jb in range(d // CONV_TN):
        lo = jb * CONV_TN
        cols = slice(lo, lo + CONV_TN)
        gate_b = _dot(h, win_ref[:, lo:lo + CONV_TN])
        gate_c = _dot(h, win_ref[:, d + lo:d + lo + CONV_TN])
        val = _dot(h, win_ref[:, 2 * d + lo:2 * d + lo + CONV_TN])
        z = gate_c * val
        z_ref[SUBLANES:SUBLANES + tm, cols] = z
        conv = (cw_ref[0:1, cols] * z_ref[SUBLANES - 2:SUBLANES - 2 + tm, cols]
                + cw_ref[1:2, cols] * z_ref[SUBLANES - 1:SUBLANES - 1 + tm, cols]
                + cw_ref[2:3, cols] * z)
        gb_ref[:, cols] = (gate_b * conv).astype(BF16)
    o_ref[...] = x + _dot(gb_ref[...], wout_ref[...])


def _short_conv(x, gain, w_in, conv_w, w_out, layer):
    m, d = x.shape
    row = lambda i: (i, 0)
    return pl.pallas_call(
        functools.partial(_conv_kernel, layer=layer),
        out_shape=jax.ShapeDtypeStruct((m, d), F32),
        grid=(m // MIX_TM,),
        in_specs=[
            pl.BlockSpec((MIX_TM, d), row),
            _resident((1, d)),
            _HBM,
            _resident((CONV_WIDTH, d)),
            _HBM,
        ],
        out_specs=pl.BlockSpec((MIX_TM, d), row),
        scratch_shapes=[*_stage_scratch((d, 3 * d)),
                        *_stage_scratch((d, d)),
                        pltpu.VMEM((MIX_TM + SUBLANES, d), F32),
                        pltpu.VMEM((MIX_TM, d), BF16)],
        compiler_params=_params(("arbitrary",)),
        name="short_conv",
    )(x, gain, w_in, conv_w, w_out)


def _kv_kernel(mem_ref, g_ref, wkv_ref, kv_ref):
    mem_n = _rms(mem_ref[...], g_ref[...]).astype(BF16)
    kv_ref[...] = _dot(mem_n, wkv_ref[...].astype(BF16)).astype(BF16)


def _kv_proj(mem, gain, wkv, layer):
    mlen, d = mem.shape
    n = wkv.shape[2]
    return pl.pallas_call(
        _kv_kernel,
        out_shape=jax.ShapeDtypeStruct((mlen, n), BF16),
        grid=(n // KV_TN,),
        in_specs=[
            pl.BlockSpec((mlen, d), lambda j: (0, 0)),
            pl.BlockSpec((1, d), lambda j: (0, 0)),
            pl.BlockSpec((None, d, KV_TN), lambda j: (layer, 0, j)),
        ],
        out_specs=pl.BlockSpec((mlen, KV_TN), lambda j: (0, j)),
        compiler_params=_params(("parallel",)),
        name="kv_proj",
    )(mem, gain, wkv)


def _xattn_kernel(x_ref, xn_ref, g_ref, wq_hbm, k_ref, v_ref, wo_hbm, o_ref,
                  wq_ref, wo_ref, stage, sem, qa_ref, qb_ref, oa_ref, ob_ref, *, layer):
    hd = XATTN_HEAD_DIM
    tm = ATT_TM

    def normed(x):
        return _rms(x, g_ref[...]).astype(BF16)

    def attend(q_ref, x, ob_scr, h_next, q_next):
        for hh in range(XATTN_HEADS):
            cols = slice(hh * hd, (hh + 1) * hd)
            s = lax.dot_general(q_ref[:, cols], k_ref[:, cols], (((1,), (1,)), ((), ())),
                                preferred_element_type=F32) * (hd ** -0.5)
            q_next[:, cols] = _dot(h_next, wq_ref[:, cols]).astype(BF16)
            p = jnp.exp(s - jnp.max(s, axis=-1, keepdims=True))
            p = p / jnp.sum(p, axis=-1, keepdims=True)
            ob_scr[:, cols] = _dot(p.astype(BF16), v_ref[:, cols]).astype(BF16)
        return x + _dot(ob_scr[...], wo_ref[...])

    @pl.when(pl.program_id(0) == 0)
    def _():
        _load_weight(wq_hbm.at[layer], wq_ref, stage, sem)
        _load_weight(wo_hbm.at[layer], wo_ref, stage, sem)
        qa_ref[...] = _dot(normed(x_ref[0:tm, :]), wq_ref[...]).astype(BF16)

    x0 = x_ref[0:tm, :]
    x1 = x_ref[tm:2 * tm, :]
    o_ref[0:tm, :] = attend(qa_ref, x0, oa_ref, normed(x1), qb_ref)
    o_ref[tm:2 * tm, :] = attend(qb_ref, x1, ob_ref, normed(xn_ref[...]), qa_ref)


def _xattn(x, gain, wq, kv, wo, layer):
    m, d = x.shape
    mlen = kv.shape[0]
    n = m // (2 * ATT_TM)
    row = lambda i: (i, 0)
    w_scratch, stage, sem = _stage_scratch((d, d))
    half = pltpu.VMEM((ATT_TM, d), BF16)
    return pl.pallas_call(
        functools.partial(_xattn_kernel, layer=layer),
        out_shape=jax.ShapeDtypeStruct((m, d), F32),
        grid=(n,),
        in_specs=[
            pl.BlockSpec((2 * ATT_TM, d), row),
            pl.BlockSpec((ATT_TM, d), lambda i: (jnp.minimum(2 * i + 2, 2 * n - 1), 0)),
            _resident((1, d)),
            _HBM,
            pl.BlockSpec((mlen, d), lambda i: (0, 0), pipeline_mode=pl.Buffered(1)),
            pl.BlockSpec((mlen, d), lambda i: (0, 1), pipeline_mode=pl.Buffered(1)),
            _HBM,
        ],
        out_specs=pl.BlockSpec((2 * ATT_TM, d), row),
        scratch_shapes=[w_scratch, w_scratch, stage, sem, half, half, half, half],
        compiler_params=_params(("arbitrary",)),
        name="xattn",
    )(x, x, gain, wq, kv, kv, wo)


def kernel(x, mem, ffn1_norm, ffn1_w13, ffn1_w2, mix_norm, gmlp_w_in, gmlp_ln_g, gmlp_ln_b, gmlp_w_s, gmlp_b_s, gmlp_w_out, conv_w_in, conv_w, conv_w_out, xattn_norm, mem_norm, xattn_wq, xattn_wkv, xattn_wo, ffn2_norm, ffn2_w13, ffn2_w2, final_norm):
    bsz, seq, d = x.shape
    assert (bsz, seq, d) == (1, SEQ, D_MODEL), "conv carry assumes one sequence laid out along rows"
    xs = x.reshape(seq, d)
    mem2 = mem.reshape(MEM_LEN, d)
    vec = lambda v: v.reshape(1, -1)

    for i in range(DEPTH):
        xs = _ffn(xs, vec(ffn1_norm[i]), ffn1_w13, ffn1_w2, i)
        j = i // 2
        if i % 2 == 0:
            xs = _gmlp(xs, vec(mix_norm[i]), gmlp_w_in, vec(gmlp_ln_g[j]), vec(gmlp_ln_b[j]),
                       gmlp_w_s[j], gmlp_b_s[j], gmlp_w_out, j)
        else:
            xs = _short_conv(xs, vec(mix_norm[i]), conv_w_in, conv_w[j], conv_w_out, j)
        kv = _kv_proj(mem2, vec(mem_norm[i]), xattn_wkv, i)
        xs = _xattn(xs, vec(xattn_norm[i]), xattn_wq, kv, xattn_wo, i)
        last = i == DEPTH - 1
        xs = _ffn(xs, vec(ffn2_norm[i]), ffn2_w13, ffn2_w2, i,
                  final_gain=vec(final_norm) if last else None)
    return xs.reshape(bsz, seq, d)
```

```python
import functools

import jax
import jax.numpy as jnp
from jax import lax
from jax.experimental import pallas as pl
from jax.experimental.pallas import tpu as pltpu

D_MODEL = 2048
SEQ = 8192
DEPTH = 2
MEM_LEN = 256
D_FF = 5632
CHUNK = 128
GMLP_WIDTH = 2048
GMLP_GROUPS = 8
GMLP_GROUP_DIM = GMLP_WIDTH // GMLP_GROUPS
CONV_WIDTH = 3
XATTN_HEADS = 4
XATTN_HEAD_DIM = D_MODEL // XATTN_HEADS
RMS_EPS = 1e-6
LN_EPS = 1e-5

F32 = jnp.float32
BF16 = jnp.bfloat16

V7X_VMEM_LIMIT_BYTES = 60 * 1024 * 1024
SUBLANES = 8

FFN_TM = 1024
FFN_TF = 512
FFN_FIRST_ROWS = FFN_TM
NORM_ROWS = 256
MIX_TM = 256
GMLP_TM = 512
ATT_TM = 256
CONV_TN = 512
KV_TN = 512
STAGE_BYTES = 2 * 1024 * 1024


def _rms(x, g):
    return x * lax.rsqrt(jnp.mean(x * x, axis=-1, keepdims=True) + RMS_EPS) * g


def _dot(a, b):
    return jnp.dot(a, b, preferred_element_type=F32)


def _params(semantics):
    return pltpu.CompilerParams(dimension_semantics=semantics,
                                vmem_limit_bytes=V7X_VMEM_LIMIT_BYTES)


def _resident(shape):
    return pl.BlockSpec(shape, lambda *_: (0,) * len(shape), pipeline_mode=pl.Buffered(1))


_HBM = pl.BlockSpec(memory_space=pl.ANY)


def _stage_rows(w_shape):
    rows = STAGE_BYTES // (w_shape[1] * 4) // 16 * 16
    while w_shape[0] % rows:
        rows -= 16
    return rows


def _stage_scratch(w_shape):
    return [pltpu.VMEM(w_shape, BF16),
            pltpu.VMEM((2, _stage_rows(w_shape), w_shape[1]), F32),
            pltpu.SemaphoreType.DMA((2,))]


def _load_weight(w_hbm, w_vmem, stage, sem, col0=0):
    rows, width = stage.shape[1], stage.shape[2]
    n = w_hbm.shape[0] // rows
    cols = slice(col0, col0 + width)

    def copy(c, slot):
        return pltpu.make_async_copy(w_hbm.at[pl.ds(c * rows, rows), cols], stage.at[slot], sem.at[slot])

    copy(0, 0).start()

    def body(c, carry):
        slot = lax.rem(c, 2)

        @pl.when(c + 1 < n)
        def _():
            copy(c + 1, 1 - slot).start()

        copy(c, slot).wait()
        w_vmem[pl.ds(pl.multiple_of(c * rows, rows), rows), cols] = stage[slot].astype(BF16)
        return carry

    lax.fori_loop(0, n, body, 0)


def _swiglu(h, w13, w2):
    tf = w13.shape[1] // 2
    gu = _dot(h, w13)
    gate = gu[:, :tf]
    up = gu[:, tf:]
    act = (gate * (0.5 / (1.0 + jnp.exp(-gate))) * up).astype(BF16)
    return _dot(act, w2)


def _ffn_first_kernel(x_hbm, g_ref, w1_ref, w3_ref, w2_ref, fg_ref, o_ref, w13b_ref, w2b_ref,
                      h_ref, sem, *, final_norm):
    j = pl.program_id(0)

    @pl.when(j == 0)
    def _():
        copy = pltpu.make_async_copy(x_hbm.at[pl.ds(0, FFN_FIRST_ROWS)], o_ref, sem)
        copy.start()
        copy.wait()
        for r in range(0, FFN_FIRST_ROWS, NORM_ROWS):
            h_ref[r:r + NORM_ROWS, :] = _rms(o_ref[r:r + NORM_ROWS, :], g_ref[...]).astype(BF16)

    w13 = jnp.concatenate([w1_ref[...].astype(BF16), w3_ref[...].astype(BF16)], axis=1)
    w2 = w2_ref[...].astype(BF16)
    w13b_ref[...] = w13
    w2b_ref[...] = w2
    o_ref[...] += _swiglu(h_ref[...], w13, w2)

    if final_norm:
        @pl.when(j == pl.num_programs(0) - 1)
        def _():
            for r in range(0, FFN_FIRST_ROWS, NORM_ROWS):
                o_ref[r:r + NORM_ROWS, :] = _rms(o_ref[r:r + NORM_ROWS, :], fg_ref[...])


def _ffn_rest_kernel(x_ref, g_ref, w13_ref, w2_ref, fg_ref, y0_hbm, o_ref, h_ref, sem,
                     *, nf, n_first, final_norm):
    s = pl.program_id(0)
    j = lax.rem(jnp.maximum(s - n_first, 0), nf)

    @pl.when(s < n_first)
    def _():
        copy = pltpu.make_async_copy(y0_hbm.at[pl.ds(pl.multiple_of(s * FFN_TM, FFN_TM), FFN_TM)], o_ref, sem)
        copy.start()
        copy.wait()

    @pl.when(s >= n_first)
    def _():
        @pl.when(j == 0)
        def _():
            x = x_ref[...]
            h_ref[...] = _rms(x, g_ref[...]).astype(BF16)
            o_ref[...] = x

        o_ref[...] += _swiglu(h_ref[...], w13_ref[...], w2_ref[...])

        if final_norm:
            @pl.when(j == nf - 1)
            def _():
                o_ref[...] = _rms(o_ref[...], fg_ref[...])


def _ffn(x, gain, w13, w2, layer, final_gain=None):
    m, d = x.shape
    f = w2.shape[1]
    final_norm = final_gain is not None
    if final_gain is None:
        final_gain = gain
    vec = pl.BlockSpec((1, d), lambda s: (0, 0))

    nf = f // FFN_TF
    y0, w13b, w2b = pl.pallas_call(
        functools.partial(_ffn_first_kernel, final_norm=final_norm),
        out_shape=[jax.ShapeDtypeStruct((FFN_FIRST_ROWS, d), F32),
                   jax.ShapeDtypeStruct((nf, d, 2 * FFN_TF), BF16),
                   jax.ShapeDtypeStruct((f, d), BF16)],
        grid=(nf,),
        in_specs=[
            _HBM,
            vec,
            pl.BlockSpec((None, d, FFN_TF), lambda j: (layer, 0, j)),
            pl.BlockSpec((None, d, FFN_TF), lambda j: (layer, 0, j + nf)),
            pl.BlockSpec((None, FFN_TF, d), lambda j: (layer, j, 0)),
            vec,
        ],
        out_specs=[
            pl.BlockSpec((FFN_FIRST_ROWS, d), lambda j: (0, 0)),
            pl.BlockSpec((None, d, 2 * FFN_TF), lambda j: (j, 0, 0)),
            pl.BlockSpec((FFN_TF, d), lambda j: (j, 0)),
        ],
        scratch_shapes=[pltpu.VMEM((FFN_FIRST_ROWS, d), BF16), pltpu.SemaphoreType.DMA(())],
        compiler_params=_params(("arbitrary",)),
        name="ffn_first",
    )(x, gain, w13, w13, w2, final_gain)

    n_first = FFN_FIRST_ROWS // FFN_TM
    n_rest = m // FFN_TM - n_first
    rest = lambda s: jnp.maximum(s - n_first, 0)
    out_row = lambda s: (jnp.where(s < n_first, s, n_first + rest(s) // nf), 0)
    x_row = lambda s: (n_first + rest(s) // nf, 0)
    w_col = lambda s: lax.rem(rest(s), nf)
    return pl.pallas_call(
        functools.partial(_ffn_rest_kernel, nf=nf, n_first=n_first, final_norm=final_norm),
        out_shape=jax.ShapeDtypeStruct((m, d), F32),
        grid=(n_first + n_rest * nf,),
        in_specs=[
            pl.BlockSpec((FFN_TM, d), x_row),
            vec,
            pl.BlockSpec((None, d, 2 * FFN_TF), lambda s: (w_col(s), 0, 0)),
            pl.BlockSpec((FFN_TF, d), lambda s: (w_col(s), 0)),
            vec,
            _HBM,
        ],
        out_specs=pl.BlockSpec((FFN_TM, d), out_row),
        scratch_shapes=[pltpu.VMEM((FFN_TM, d), BF16), pltpu.SemaphoreType.DMA(())],
        compiler_params=_params(("arbitrary",)),
        name="ffn_rest",
    )(x, gain, w13b, w2b, final_gain, y0)


def _gmlp_kernel(x_ref, g_ref, win_hbm, lng_ref, lnb_ref, ws_ref, bs_ref, wout_hbm, o_ref,
                 win_ref, wout_ref, u_ref, sem, v_ref, uf_ref, *, layer):
    e = GMLP_WIDTH
    half = GMLP_TM // 2

    @pl.when(pl.program_id(0) == 0)
    def _():
        _load_weight(win_hbm.at[layer], win_ref, u_ref, sem, col0=0)
        _load_weight(win_hbm.at[layer], win_ref, u_ref, sem, col0=e)
        _load_weight(wout_hbm.at[layer], wout_ref, u_ref, sem)

    x = x_ref[...]
    h = _rms(x, g_ref[...]).astype(BF16)

    def gelu(z):
        return 0.5 * z * (1.0 + lax.erf(z * (2.0 ** -0.5)))

    v = gelu(_dot(h, win_ref[:, e:]))
    u = gelu(_dot(h, win_ref[:, :e]))
    u_ref[0] = u[:half]
    u_ref[1] = u[half:]
    mu = jnp.mean(v, axis=-1, keepdims=True)
    vc = v - mu
    var = jnp.mean(vc * vc, axis=-1, keepdims=True)
    v_ref[...] = (vc * lax.rsqrt(var + LN_EPS) * lng_ref[...] + lnb_ref[...]).astype(BF16)

    t_idx = lax.broadcasted_iota(jnp.int32, (CHUNK, CHUNK), 0)
    s_idx = lax.broadcasted_iota(jnp.int32, (CHUNK, CHUNK), 1)
    causal = t_idx >= s_idx
    for g in range(GMLP_GROUPS):
        w = jnp.where(causal, ws_ref[g], 0.0).astype(BF16)
        cols = slice(g * GMLP_GROUP_DIM, (g + 1) * GMLP_GROUP_DIM)
        for c in range(GMLP_TM // CHUNK):
            rows = slice(c * CHUNK, (c + 1) * CHUNK)
            r0 = (c * CHUNK) % half
            f = _dot(w, v_ref[rows, cols]) + bs_ref[g]
            uf_ref[rows, cols] = (u_ref[(c * CHUNK) // half, r0:r0 + CHUNK, cols] * f).astype(BF16)
    o_ref[...] = x + _dot(uf_ref[...], wout_ref[...])


def _gmlp(x, gain, w_in, ln_g, ln_b, w_s, b_s, w_out, layer):
    m, d = x.shape
    e = GMLP_WIDTH
    row = lambda i: (i, 0)
    bias = jnp.broadcast_to(b_s[:, :, None], (GMLP_GROUPS, CHUNK, GMLP_GROUP_DIM))
    assert d == e, "one staging width serves both weights"
    return pl.pallas_call(
        functools.partial(_gmlp_kernel, layer=layer),
        out_shape=jax.ShapeDtypeStruct((m, d), F32),
        grid=(m // GMLP_TM,),
        in_specs=[
            pl.BlockSpec((GMLP_TM, d), row),
            _resident((1, d)),
            _HBM,
            _resident((1, e)),
            _resident((1, e)),
            _resident((GMLP_GROUPS, CHUNK, CHUNK)),
            _resident((GMLP_GROUPS, CHUNK, GMLP_GROUP_DIM)),
            _HBM,
        ],
        out_specs=pl.BlockSpec((GMLP_TM, d), row),
        scratch_shapes=[pltpu.VMEM((d, 2 * e), BF16),
                        pltpu.VMEM((e, d), BF16),
                        pltpu.VMEM((2, GMLP_TM // 2, e), F32),
                        pltpu.SemaphoreType.DMA((2,)),
                        pltpu.VMEM((GMLP_TM, e), BF16),
                        pltpu.VMEM((GMLP_TM, e), BF16)],
        compiler_params=_params(("arbitrary",)),
        name="gmlp",
    )(x, gain, w_in, ln_g, ln_b, w_s, bias, w_out)


def _conv_kernel(x_ref, g_ref, win_hbm, cw_ref, wout_hbm, o_ref,
                 win_ref, win_stage, win_sem, wout_ref, wout_stage, wout_sem, z_ref, gb_ref, *, layer):
    d = D_MODEL
    tm = MIX_TM
    i = pl.program_id(0)

    @pl.when(i == 0)
    def _():
        _load_weight(win_hbm.at[layer], win_ref, win_stage, win_sem)
        _load_weight(wout_hbm.at[layer], wout_ref, wout_stage, wout_sem)
        z_ref[0:SUBLANES, :] = jnp.zeros((SUBLANES, d), F32)

    @pl.when(i > 0)
    def _():
        z_ref[0:SUBLANES, :] = z_ref[tm:tm + SUBLANES, :]

    x = x_ref[...]
    h = _rms(x, g_ref[...]).astype(BF16)
    for jb in range(d // CONV_TN):
        lo = jb * CONV_TN
        cols = slice(lo, lo + CONV_TN)
        gate_b = _dot(h, win_ref[:, lo:lo + CONV_TN])
        gate_c = _dot(h, win_ref[:, d + lo:d + lo + CONV_TN])
        val = _dot(h, win_ref[:, 2 * d + lo:2 * d + lo + CONV_TN])
        z = gate_c * val
        z_ref[SUBLANES:SUBLANES + tm, cols] = z
        conv = (cw_ref[0:1, cols] * z_ref[SUBLANES - 2:SUBLANES - 2 + tm, cols]
                + cw_ref[1:2, cols] * z_ref[SUBLANES - 1:SUBLANES - 1 + tm, cols]
                + cw_ref[2:3, cols] * z)
        gb_ref[:, cols] = (gate_b * conv).astype(BF16)
    o_ref[...] = x + _dot(gb_ref[...], wout_ref[...])


def _short_conv(x, gain, w_in, conv_w, w_out, layer):
    m, d = x.shape
    row = lambda i: (i, 0)
    return pl.pallas_call(
        functools.partial(_conv_kernel, layer=layer),
        out_shape=jax.ShapeDtypeStruct((m, d), F32),
        grid=(m // MIX_TM,),
        in_specs=[
            pl.BlockSpec((MIX_TM, d), row),
            _resident((1, d)),
            _HBM,
            _resident((CONV_WIDTH, d)),
            _HBM,
        ],
        out_specs=pl.BlockSpec((MIX_TM, d), row),
        scratch_shapes=[*_stage_scratch((d, 3 * d)),
                        *_stage_scratch((d, d)),
                        pltpu.VMEM((MIX_TM + SUBLANES, d), F32),
                        pltpu.VMEM((MIX_TM, d), BF16)],
        compiler_params=_params(("arbitrary",)),
        name="short_conv",
    )(x, gain, w_in, conv_w, w_out)


def _kv_kernel(mem_ref, g_ref, wkv_ref, kv_ref):
    mem_n = _rms(mem_ref[...], g_ref[...]).astype(BF16)
    kv_ref[...] = _dot(mem_n, wkv_ref[...].astype(BF16)).astype(BF16)


def _kv_proj(mem, gain, wkv, layer):
    mlen, d = mem.shape
    n = wkv.shape[2]
    return pl.pallas_call(
        _kv_kernel,
        out_shape=jax.ShapeDtypeStruct((mlen, n), BF16),
        grid=(n // KV_TN,),
        in_specs=[
            pl.BlockSpec((mlen, d), lambda j: (0, 0)),
            pl.BlockSpec((1, d), lambda j: (0, 0)),
            pl.BlockSpec((None, d, KV_TN), lambda j: (layer, 0, j)),
        ],
        out_specs=pl.BlockSpec((mlen, KV_TN), lambda j: (0, j)),
        compiler_params=_params(("parallel",)),
        name="kv_proj",
    )(mem, gain, wkv)


def _xattn_kernel(x_ref, xn_ref, g_ref, wq_hbm, k_ref, v_ref, wo_hbm, o_ref,
                  wq_ref, wo_ref, stage, sem, qa_ref, qb_ref, oa_ref, ob_ref, *, layer):
    hd = XATTN_HEAD_DIM
    tm = ATT_TM

    def normed(x):
        return _rms(x, g_ref[...]).astype(BF16)

    def attend(q_ref, x, ob_scr, h_next, q_next):
        for hh in range(XATTN_HEADS):
            cols = slice(hh * hd, (hh + 1) * hd)
            s = lax.dot_general(q_ref[:, cols], k_ref[:, cols], (((1,), (1,)), ((), ())),
                                preferred_element_type=F32) * (hd ** -0.5)
            q_next[:, cols] = _dot(h_next, wq_ref[:, cols]).astype(BF16)
            p = jnp.exp(s - jnp.max(s, axis=-1, keepdims=True))
            p = p / jnp.sum(p, axis=-1, keepdims=True)
            ob_scr[:, cols] = _dot(p.astype(BF16), v_ref[:, cols]).astype(BF16)
        return x + _dot(ob_scr[...], wo_ref[...])

    @pl.when(pl.program_id(0) == 0)
    def _():
        _load_weight(wq_hbm.at[layer], wq_ref, stage, sem)
        _load_weight(wo_hbm.at[layer], wo_ref, stage, sem)
        qa_ref[...] = _dot(normed(x_ref[0:tm, :]), wq_ref[...]).astype(BF16)

    x0 = x_ref[0:tm, :]
    x1 = x_ref[tm:2 * tm, :]
    o_ref[0:tm, :] = attend(qa_ref, x0, oa_ref, normed(x1), qb_ref)
    o_ref[tm:2 * tm, :] = attend(qb_ref, x1, ob_ref, normed(xn_ref[...]), qa_ref)


def _xattn(x, gain, wq, kv, wo, layer):
    m, d = x.shape
    mlen = kv.shape[0]
    n = m // (2 * ATT_TM)
    row = lambda i: (i, 0)
    w_scratch, stage, sem = _stage_scratch((d, d))
    half = pltpu.VMEM((ATT_TM, d), BF16)
    return pl.pallas_call(
        functools.partial(_xattn_kernel, layer=layer),
        out_shape=jax.ShapeDtypeStruct((m, d), F32),
        grid=(n,),
        in_specs=[
            pl.BlockSpec((2 * ATT_TM, d), row),
            pl.BlockSpec((ATT_TM, d), lambda i: (jnp.minimum(2 * i + 2, 2 * n - 1), 0)),
            _resident((1, d)),
            _HBM,
            pl.BlockSpec((mlen, d), lambda i: (0, 0), pipeline_mode=pl.Buffered(1)),
            pl.BlockSpec((mlen, d), lambda i: (0, 1), pipeline_mode=pl.Buffered(1)),
            _HBM,
        ],
        out_specs=pl.BlockSpec((2 * ATT_TM, d), row),
        scratch_shapes=[w_scratch, w_scratch, stage, sem, half, half, half, half],
        compiler_params=_params(("arbitrary",)),
        name="xattn",
    )(x, x, gain, wq, kv, kv, wo)


def kernel(x, mem, ffn1_norm, ffn1_w13, ffn1_w2, mix_norm, gmlp_w_in, gmlp_ln_g, gmlp_ln_b, gmlp_w_s, gmlp_b_s, gmlp_w_out, conv_w_in, conv_w, conv_w_out, xattn_norm, mem_norm, xattn_wq, xattn_wkv, xattn_wo, ffn2_norm, ffn2_w13, ffn2_w2, final_norm):
    bsz, seq, d = x.shape
    assert (bsz, seq, d) == (1, SEQ, D_MODEL), "conv carry assumes one sequence laid out along rows"
    xs = x.reshape(seq, d)
    mem2 = mem.reshape(MEM_LEN, d)
    vec = lambda v: v.reshape(1, -1)

    for i in range(DEPTH):
        xs = _ffn(xs, vec(ffn1_norm[i]), ffn1_w13, ffn1_w2, i)
        j = i // 2
        if i % 2 == 0:
            xs = _gmlp(xs, vec(mix_norm[i]), gmlp_w_in, vec(gmlp_ln_g[j]), vec(gmlp_ln_b[j]),
                       gmlp_w_s[j], gmlp_b_s[j], gmlp_w_out, j)
        else:
            xs = _short_conv(xs, vec(mix_norm[i]), conv_w_in, conv_w[j], conv_w_out, j)
        kv = _kv_proj(mem2, vec(mem_norm[i]), xattn_wkv, i)
        xs = _xattn(xs, vec(xattn_norm[i]), xattn_wq, kv, xattn_wo, i)
        last = i == DEPTH - 1
        xs = _ffn(xs, vec(ffn2_norm[i]), ffn2_w13, ffn2_w2, i,
                  final_gain=vec(final_norm) if last else None)
    return xs.reshape(bsz, seq, d)
```

```python
import functools

import jax
import jax.numpy as jnp
from jax import lax
from jax.experimental import pallas as pl
from jax.experimental.pallas import tpu as pltpu

D_MODEL = 2048
SEQ = 8192
DEPTH = 2
MEM_LEN = 256
D_FF = 5632
CHUNK = 128
GMLP_WIDTH = 2048
GMLP_GROUPS = 8
GMLP_GROUP_DIM = GMLP_WIDTH // GMLP_GROUPS
CONV_WIDTH = 3
XATTN_HEADS = 4
XATTN_HEAD_DIM = D_MODEL // XATTN_HEADS
RMS_EPS = 1e-6
LN_EPS = 1e-5

F32 = jnp.float32
BF16 = jnp.bfloat16

V7X_VMEM_LIMIT_BYTES = 60 * 1024 * 1024
SUBLANES = 8

FFN_TM = 1024
FFN_TF = 512
FFN_FIRST_ROWS = FFN_TM
NORM_ROWS = 256
MIX_TM = 256
GMLP_TM = 512
ATT_TM = 256
CONV_TN = 512
KV_TN = 1024
STAGE_BYTES = 1024 * 1024
STAGE_SLOTS = 4


def _rms(x, g):
    return x * lax.rsqrt(jnp.mean(x * x, axis=-1, keepdims=True) + RMS_EPS) * g


def _dot(a, b):
    return jnp.dot(a, b, preferred_element_type=F32)


def _params(semantics):
    return pltpu.CompilerParams(dimension_semantics=semantics,
                                vmem_limit_bytes=V7X_VMEM_LIMIT_BYTES)


def _resident(shape):
    return pl.BlockSpec(shape, lambda *_: (0,) * len(shape), pipeline_mode=pl.Buffered(1))


_HBM = pl.BlockSpec(memory_space=pl.ANY)


def _stage_rows(w_shape):
    rows = STAGE_BYTES // (w_shape[1] * 4) // 16 * 16
    while w_shape[0] % rows:
        rows -= 16
    return rows


def _stage_scratch(w_shape):
    return [pltpu.VMEM(w_shape, BF16),
            pltpu.VMEM((STAGE_SLOTS, _stage_rows(w_shape), w_shape[1]), F32),
            pltpu.SemaphoreType.DMA((STAGE_SLOTS,))]


def _load_weight(w_hbm, w_vmem, stage, sem, col0=0):
    assert stage.shape[0] == STAGE_SLOTS == 4
    rows, width = stage.shape[1], stage.shape[2]
    n_pairs = w_hbm.shape[0] // (2 * rows)
    cols = slice(col0, col0 + width)

    def copy(c, slot):
        return pltpu.make_async_copy(w_hbm.at[pl.ds(c * rows, rows), cols], stage.at[slot], sem.at[slot])

    def start_pair(p, base):
        copy(2 * p, base).start(priority=0)
        copy(2 * p + 1, base + 1).start(priority=1)

    start_pair(0, 0)

    def body(p, carry):
        base = 2 * lax.rem(p, 2)

        @pl.when(p + 1 < n_pairs)
        def _():
            start_pair(p + 1, 2 - base)

        for k in range(2):
            c = 2 * p + k
            copy(c, base + k).wait()
            w_vmem[pl.ds(pl.multiple_of(c * rows, rows), rows), cols] = stage[base + k].astype(BF16)
        return carry

    lax.fori_loop(0, n_pairs, body, 0)


def _swiglu(h, w13, w2):
    tf = w13.shape[1] // 2
    gu = _dot(h, w13)
    gate = gu[:, :tf]
    up = gu[:, tf:]
    act = (gate * (0.5 / (1.0 + jnp.exp(-gate))) * up).astype(BF16)
    return _dot(act, w2)


def _ffn_first_kernel(x_hbm, g_ref, w1_ref, w3_ref, w2_ref, fg_ref, o_ref, w13b_ref, w2b_ref,
                      h_ref, sem, *, final_norm):
    j = pl.program_id(0)

    @pl.when(j == 0)
    def _():
        copy = pltpu.make_async_copy(x_hbm.at[pl.ds(0, FFN_FIRST_ROWS)], o_ref, sem)
        copy.start()
        copy.wait()
        for r in range(0, FFN_FIRST_ROWS, NORM_ROWS):
            h_ref[r:r + NORM_ROWS, :] = _rms(o_ref[r:r + NORM_ROWS, :], g_ref[...]).astype(BF16)

    w13 = jnp.concatenate([w1_ref[...].astype(BF16), w3_ref[...].astype(BF16)], axis=1)
    w2 = w2_ref[...].astype(BF16)
    w13b_ref[...] = w13
    w2b_ref[...] = w2
    o_ref[...] += _swiglu(h_ref[...], w13, w2)

    if final_norm:
        @pl.when(j == pl.num_programs(0) - 1)
        def _():
            for r in range(0, FFN_FIRST_ROWS, NORM_ROWS):
                o_ref[r:r + NORM_ROWS, :] = _rms(o_ref[r:r + NORM_ROWS, :], fg_ref[...])


def _ffn_rest_kernel(x_ref, g_ref, w13_ref, w2_ref, fg_ref, y0_hbm, o_ref, h_ref, sem,
                     *, nf, n_first, final_norm):
    s = pl.program_id(0)
    j = lax.rem(jnp.maximum(s - n_first, 0), nf)

    @pl.when(s < n_first)
    def _():
        copy = pltpu.make_async_copy(y0_hbm.at[pl.ds(pl.multiple_of(s * FFN_TM, FFN_TM), FFN_TM)], o_ref, sem)
        copy.start()
        copy.wait()

    @pl.when(s >= n_first)
    def _():
        @pl.when(j == 0)
        def _():
            x = x_ref[...]
            h_ref[...] = _rms(x, g_ref[...]).astype(BF16)
            o_ref[...] = x

        o_ref[...] += _swiglu(h_ref[...], w13_ref[...], w2_ref[...])

        if final_norm:
            @pl.when(j == nf - 1)
            def _():
                o_ref[...] = _rms(o_ref[...], fg_ref[...])


def _ffn(x, gain, w13, w2, layer, final_gain=None):
    m, d = x.shape
    f = w2.shape[1]
    final_norm = final_gain is not None
    if final_gain is None:
        final_gain = gain
    vec = pl.BlockSpec((1, d), lambda s: (0, 0))

    nf = f // FFN_TF
    y0, w13b, w2b = pl.pallas_call(
        functools.partial(_ffn_first_kernel, final_norm=final_norm),
        out_shape=[jax.ShapeDtypeStruct((FFN_FIRST_ROWS, d), F32),
                   jax.ShapeDtypeStruct((nf, d, 2 * FFN_TF), BF16),
                   jax.ShapeDtypeStruct((f, d), BF16)],
        grid=(nf,),
        in_specs=[
            _HBM,
            vec,
            pl.BlockSpec((None, d, FFN_TF), lambda j: (layer, 0, j)),
            pl.BlockSpec((None, d, FFN_TF), lambda j: (layer, 0, j + nf)),
            pl.BlockSpec((None, FFN_TF, d), lambda j: (layer, j, 0)),
            vec,
        ],
        out_specs=[
            pl.BlockSpec((FFN_FIRST_ROWS, d), lambda j: (0, 0)),
            pl.BlockSpec((None, d, 2 * FFN_TF), lambda j: (j, 0, 0)),
            pl.BlockSpec((FFN_TF, d), lambda j: (j, 0)),
        ],
        scratch_shapes=[pltpu.VMEM((FFN_FIRST_ROWS, d), BF16), pltpu.SemaphoreType.DMA(())],
        compiler_params=_params(("arbitrary",)),
        name="ffn_first",
    )(x, gain, w13, w13, w2, final_gain)

    n_first = FFN_FIRST_ROWS // FFN_TM
    n_rest = m // FFN_TM - n_first
    rest = lambda s: jnp.maximum(s - n_first, 0)
    out_row = lambda s: (jnp.where(s < n_first, s, n_first + rest(s) // nf), 0)
    x_row = lambda s: (n_first + rest(s) // nf, 0)
    w_col = lambda s: lax.rem(rest(s), nf)
    return pl.pallas_call(
        functools.partial(_ffn_rest_kernel, nf=nf, n_first=n_first, final_norm=final_norm),
        out_shape=jax.ShapeDtypeStruct((m, d), F32),
        grid=(n_first + n_rest * nf,),
        in_specs=[
            pl.BlockSpec((FFN_TM, d), x_row),
            vec,
            pl.BlockSpec((None, d, 2 * FFN_TF), lambda s: (w_col(s), 0, 0)),
            pl.BlockSpec((FFN_TF, d), lambda s: (w_col(s), 0)),
            vec,
            _HBM,
        ],
        out_specs=pl.BlockSpec((FFN_TM, d), out_row),
        scratch_shapes=[pltpu.VMEM((FFN_TM, d), BF16), pltpu.SemaphoreType.DMA(())],
        compiler_params=_params(("arbitrary",)),
        name="ffn_rest",
    )(x, gain, w13b, w2b, final_gain, y0)


def _gmlp_kernel(x_ref, g_ref, win_hbm, lng_ref, lnb_ref, ws_ref, bs_ref, wout_hbm, o_ref,
                 win_ref, wout_ref, u_ref, sem, v_ref, uf_ref, *, layer):
    e = GMLP_WIDTH

    @pl.when(pl.program_id(0) == 0)
    def _():
        _load_weight(win_hbm.at[layer], win_ref, u_ref, sem, col0=0)
        _load_weight(win_hbm.at[layer], win_ref, u_ref, sem, col0=e)
        _load_weight(wout_hbm.at[layer], wout_ref, u_ref, sem)

    x = x_ref[...]
    h = _rms(x, g_ref[...]).astype(BF16)

    def gelu(z):
        return 0.5 * z * (1.0 + lax.erf(z * (2.0 ** -0.5)))

    v = gelu(_dot(h, win_ref[:, e:]))
    u = gelu(_dot(h, win_ref[:, :e]))
    for c in range(GMLP_TM // CHUNK):
        u_ref[c] = u[c * CHUNK:(c + 1) * CHUNK]
    mu = jnp.mean(v, axis=-1, keepdims=True)
    vc = v - mu
    var = jnp.mean(vc * vc, axis=-1, keepdims=True)
    v_ref[...] = (vc * lax.rsqrt(var + LN_EPS) * lng_ref[...] + lnb_ref[...]).astype(BF16)

    t_idx = lax.broadcasted_iota(jnp.int32, (CHUNK, CHUNK), 0)
    s_idx = lax.broadcasted_iota(jnp.int32, (CHUNK, CHUNK), 1)
    causal = t_idx >= s_idx
    for g in range(GMLP_GROUPS):
        w = jnp.where(causal, ws_ref[g], 0.0).astype(BF16)
        cols = slice(g * GMLP_GROUP_DIM, (g + 1) * GMLP_GROUP_DIM)
        for c in range(GMLP_TM // CHUNK):
            rows = slice(c * CHUNK, (c + 1) * CHUNK)
            f = _dot(w, v_ref[rows, cols]) + bs_ref[g]
            uf_ref[rows, cols] = (u_ref[c, :, cols] * f).astype(BF16)
    o_ref[...] = x + _dot(uf_ref[...], wout_ref[...])


def _gmlp(x, gain, w_in, ln_g, ln_b, w_s, b_s, w_out, layer):
    m, d = x.shape
    e = GMLP_WIDTH
    row = lambda i: (i, 0)
    bias = jnp.broadcast_to(b_s[:, :, None], (GMLP_GROUPS, CHUNK, GMLP_GROUP_DIM))
    assert d == e and GMLP_TM == STAGE_SLOTS * CHUNK, "the u slabs double as the weight staging slots"
    return pl.pallas_call(
        functools.partial(_gmlp_kernel, layer=layer),
        out_shape=jax.ShapeDtypeStruct((m, d), F32),
        grid=(m // GMLP_TM,),
        in_specs=[
            pl.BlockSpec((GMLP_TM, d), row),
            _resident((1, d)),
            _HBM,
            _resident((1, e)),
            _resident((1, e)),
            _resident((GMLP_GROUPS, CHUNK, CHUNK)),
            _resident((GMLP_GROUPS, CHUNK, GMLP_GROUP_DIM)),
            _HBM,
        ],
        out_specs=pl.BlockSpec((GMLP_TM, d), row),
        scratch_shapes=[pltpu.VMEM((d, 2 * e), BF16),
                        pltpu.VMEM((e, d), BF16),
                        pltpu.VMEM((STAGE_SLOTS, CHUNK, e), F32),
                        pltpu.SemaphoreType.DMA((STAGE_SLOTS,)),
                        pltpu.VMEM((GMLP_TM, e), BF16),
                        pltpu.VMEM((GMLP_TM, e), BF16)],
        compiler_params=_params(("arbitrary",)),
        name="gmlp",
    )(x, gain, w_in, ln_g, ln_b, w_s, bias, w_out)


def _conv_kernel(x_ref, g_ref, win_hbm, cw_ref, wout_hbm, o_ref,
                 win_ref, win_stage, win_sem, wout_ref, wout_stage, wout_sem, z_ref, gb_ref, *, layer):
    d = D_MODEL
    tm = MIX_TM
    i = pl.program_id(0)

    @pl.when(i == 0)
    def _():
        _load_weight(win_hbm.at[layer], win_ref, win_stage, win_sem)
        _load_weight(wout_hbm.at[layer], wout_ref, wout_stage, wout_sem)
        z_ref[0:SUBLANES, :] = jnp.zeros((SUBLANES, d), F32)

    @pl.when(i > 0)
    def _():
        z_ref[0:SUBLANES, :] = z_ref[tm:tm + SUBLANES, :]

    x = x_ref[...]
    h = _rms(x, g_ref[...]).astype(BF16)
    for jb in range(d // CONV_TN):
        lo = jb * CONV_TN
        cols = slice(lo, lo + CONV_TN)
        gate_b = _dot(h, win_ref[:, lo:lo + CONV_TN])
        gate_c = _dot(h, win_ref[:, d + lo:d + lo + CONV_TN])
        val = _dot(h, win_ref[:, 2 * d + lo:2 * d + lo + CONV_TN])
        z = gate_c * val
        z_ref[SUBLANES:SUBLANES + tm, cols] = z
        conv = (cw_ref[0:1, cols] * z_ref[SUBLANES - 2:SUBLANES - 2 + tm, cols]
                + cw_ref[1:2, cols] * z_ref[SUBLANES - 1:SUBLANES - 1 + tm, cols]
                + cw_ref[2:3, cols] * z)
        gb_ref[:, cols] = (gate_b * conv).astype(BF16)
    o_ref[...] = x + _dot(gb_ref[...], wout_ref[...])


def _short_conv(x, gain, w_in, conv_w, w_out, layer):
    m, d = x.shape
    row = lambda i: (i, 0)
    return pl.pallas_call(
        functools.partial(_conv_kernel, layer=layer),
        out_shape=jax.ShapeDtypeStruct((m, d), F32),
        grid=(m // MIX_TM,),
        in_specs=[
            pl.BlockSpec((MIX_TM, d), row),
            _resident((1, d)),
            _HBM,
            _resident((CONV_WIDTH, d)),
            _HBM,
        ],
        out_specs=pl.BlockSpec((MIX_TM, d), row),
        scratch_shapes=[*_stage_scratch((d, 3 * d)),
                        *_stage_scratch((d, d)),
                        pltpu.VMEM((MIX_TM + SUBLANES, d), F32),
                        pltpu.VMEM((MIX_TM, d), BF16)],
        compiler_params=_params(("arbitrary",)),
        name="short_conv",
    )(x, gain, w_in, conv_w, w_out)


def _kv_kernel(mem_ref, g_ref, wkv_ref, kv_ref):
    mem_n = _rms(mem_ref[...], g_ref[...]).astype(BF16)
    kv_ref[...] = _dot(mem_n, wkv_ref[...].astype(BF16)).astype(BF16)


def _kv_proj(mem, gain, wkv, layer):
    mlen, d = mem.shape
    n = wkv.shape[2]
    return pl.pallas_call(
        _kv_kernel,
        out_shape=jax.ShapeDtypeStruct((mlen, n), BF16),
        grid=(n // KV_TN,),
        in_specs=[
            pl.BlockSpec((mlen, d), lambda j: (0, 0)),
            pl.BlockSpec((1, d), lambda j: (0, 0)),
            pl.BlockSpec((None, d, KV_TN), lambda j: (layer, 0, j)),
        ],
        out_specs=pl.BlockSpec((mlen, KV_TN), lambda j: (0, j)),
        compiler_params=_params(("parallel",)),
        name="kv_proj",
    )(mem, gain, wkv)


def _xattn_kernel(x_ref, xn_ref, g_ref, wq_hbm, k_ref, v_ref, wo_hbm, o_ref,
                  wq_ref, wo_ref, stage, sem, qa_ref, qb_ref, oa_ref, ob_ref, *, layer):
    hd = XATTN_HEAD_DIM
    tm = ATT_TM

    def normed(x):
        return _rms(x, g_ref[...]).astype(BF16)

    def attend(q_ref, x, ob_scr, h_next, q_next):
        for hh in range(XATTN_HEADS):
            cols = slice(hh * hd, (hh + 1) * hd)
            s = lax.dot_general(q_ref[:, cols], k_ref[:, cols], (((1,), (1,)), ((), ())),
                                preferred_element_type=F32) * (hd ** -0.5)
            q_next[:, cols] = _dot(h_next, wq_ref[:, cols]).astype(BF16)
            p = jnp.exp(s - jnp.max(s, axis=-1, keepdims=True))
            p = p / jnp.sum(p, axis=-1, keepdims=True)
            ob_scr[:, cols] = _dot(p.astype(BF16), v_ref[:, cols]).astype(BF16)
        return x + _dot(ob_scr[...], wo_ref[...])

    @pl.when(pl.program_id(0) == 0)
    def _():
        _load_weight(wq_hbm.at[layer], wq_ref, stage, sem)
        _load_weight(wo_hbm.at[layer], wo_ref, stage, sem)
        qa_ref[...] = _dot(normed(x_ref[0:tm, :]), wq_ref[...]).astype(BF16)

    x0 = x_ref[0:tm, :]
    x1 = x_ref[tm:2 * tm, :]
    o_ref[0:tm, :] = attend(qa_ref, x0, oa_ref, normed(x1), qb_ref)
    o_ref[tm:2 * tm, :] = attend(qb_ref, x1, ob_ref, normed(xn_ref[...]), qa_ref)


def _xattn(x, gain, wq, kv, wo, layer):
    m, d = x.shape
    mlen = kv.shape[0]
    n = m // (2 * ATT_TM)
    row = lambda i: (i, 0)
    w_scratch, stage, sem = _stage_scratch((d, d))
    half = pltpu.VMEM((ATT_TM, d), BF16)
    return pl.pallas_call(
        functools.partial(_xattn_kernel, layer=layer),
        out_shape=jax.ShapeDtypeStruct((m, d), F32),
        grid=(n,),
        in_specs=[
            pl.BlockSpec((2 * ATT_TM, d), row),
            pl.BlockSpec((ATT_TM, d), lambda i: (jnp.minimum(2 * i + 2, 2 * n - 1), 0)),
            _resident((1, d)),
            _HBM,
            pl.BlockSpec((mlen, d), lambda i: (0, 0), pipeline_mode=pl.Buffered(1)),
            pl.BlockSpec((mlen, d), lambda i: (0, 1), pipeline_mode=pl.Buffered(1)),
            _HBM,
        ],
        out_specs=pl.BlockSpec((2 * ATT_TM, d), row),
        scratch_shapes=[w_scratch, w_scratch, stage, sem, half, half, half, half],
        compiler_params=_params(("arbitrary",)),
        name="xattn",
    )(x, x, gain, wq, kv, kv, wo)


def kernel(x, mem, ffn1_norm, ffn1_w13, ffn1_w2, mix_norm, gmlp_w_in, gmlp_ln_g, gmlp_ln_b, gmlp_w_s, gmlp_b_s, gmlp_w_out, conv_w_in, conv_w, conv_w_out, xattn_norm, mem_norm, xattn_wq, xattn_wkv, xattn_wo, ffn2_norm, ffn2_w13, ffn2_w2, final_norm):
    bsz, seq, d = x.shape
    assert (bsz, seq, d) == (1, SEQ, D_MODEL), "conv carry assumes one sequence laid out along rows"
    xs = x.reshape(seq, d)
    mem2 = mem.reshape(MEM_LEN, d)
    vec = lambda v: v.reshape(1, -1)

    for i in range(DEPTH):
        xs = _ffn(xs, vec(ffn1_norm[i]), ffn1_w13, ffn1_w2, i)
        j = i // 2
        if i % 2 == 0:
            xs = _gmlp(xs, vec(mix_norm[i]), gmlp_w_in, vec(gmlp_ln_g[j]), vec(gmlp_ln_b[j]),
                       gmlp_w_s[j], gmlp_b_s[j], gmlp_w_out, j)
        else:
            xs = _short_conv(xs, vec(mix_norm[i]), conv_w_in, conv_w[j], conv_w_out, j)
        kv = _kv_proj(mem2, vec(mem_norm[i]), xattn_wkv, i)
        xs = _xattn(xs, vec(xattn_norm[i]), xattn_wq, kv, xattn_wo, i)
        last = i == DEPTH - 1
        xs = _ffn(xs, vec(ffn2_norm[i]), ffn2_w13, ffn2_w2, i,
                  final_gain=vec(final_norm) if last else None)
    return xs.reshape(bsz, seq, d)
```

```python
import functools

import jax
import jax.numpy as jnp
from jax import lax
from jax.experimental import pallas as pl
from jax.experimental.pallas import tpu as pltpu

D_MODEL = 2048
SEQ = 8192
DEPTH = 2
MEM_LEN = 256
D_FF = 5632
CHUNK = 128
GMLP_WIDTH = 2048
GMLP_GROUPS = 8
GMLP_GROUP_DIM = GMLP_WIDTH // GMLP_GROUPS
CONV_WIDTH = 3
XATTN_HEADS = 4
XATTN_HEAD_DIM = D_MODEL // XATTN_HEADS
RMS_EPS = 1e-6
LN_EPS = 1e-5

F32 = jnp.float32
BF16 = jnp.bfloat16

V7X_VMEM_LIMIT_BYTES = 60 * 1024 * 1024
SUBLANES = 8

FFN_TM = 1024
FFN_TF = 512
FFN_FIRST_ROWS = FFN_TM
NORM_ROWS = 256
MIX_TM = 256
GMLP_TM = 512
ATT_TM = 256
CONV_TN = 512
KV_TN = 1024
STAGE_BYTES = 2 * 1024 * 1024


def _rms(x, g):
    return x * lax.rsqrt(jnp.mean(x * x, axis=-1, keepdims=True) + RMS_EPS) * g


def _dot(a, b):
    return jnp.dot(a, b, preferred_element_type=F32)


def _params(semantics):
    return pltpu.CompilerParams(dimension_semantics=semantics,
                                vmem_limit_bytes=V7X_VMEM_LIMIT_BYTES)


def _resident(shape):
    return pl.BlockSpec(shape, lambda *_: (0,) * len(shape), pipeline_mode=pl.Buffered(1))


_HBM = pl.BlockSpec(memory_space=pl.ANY)


def _stage_rows(w_shape):
    rows = STAGE_BYTES // (w_shape[1] * 4) // 16 * 16
    while w_shape[0] % rows:
        rows -= 16
    return rows


def _stage_scratch(w_shape):
    return [pltpu.VMEM(w_shape, BF16),
            pltpu.VMEM((2, _stage_rows(w_shape), w_shape[1]), F32),
            pltpu.SemaphoreType.DMA((2,))]


def _load_weight(w_hbm, w_vmem, stage, sem, col0=0):
    rows, width = stage.shape[1], stage.shape[2]
    n = w_hbm.shape[0] // rows
    cols = slice(col0, col0 + width)

    def copy(c, slot):
        return pltpu.make_async_copy(w_hbm.at[pl.ds(c * rows, rows), cols], stage.at[slot], sem.at[slot])

    copy(0, 0).start()

    def body(c, carry):
        slot = lax.rem(c, 2)

        @pl.when(c + 1 < n)
        def _():
            copy(c + 1, 1 - slot).start()

        copy(c, slot).wait()
        w_vmem[pl.ds(pl.multiple_of(c * rows, rows), rows), cols] = stage[slot].astype(BF16)
        return carry

    lax.fori_loop(0, n, body, 0)


def _swiglu(h, w13, w2):
    tf = w13.shape[1] // 2
    gu = _dot(h, w13)
    gate = gu[:, :tf]
    up = gu[:, tf:]
    act = (gate * (0.5 / (1.0 + jnp.exp(-gate))) * up).astype(BF16)
    return _dot(act, w2)


def _ffn_first_kernel(x_hbm, g_ref, w1_ref, w3_ref, w2_ref, fg_ref, o_ref, w13b_ref, w2b_ref,
                      h_ref, sem, *, final_norm):
    j = pl.program_id(0)

    @pl.when(j == 0)
    def _():
        copy = pltpu.make_async_copy(x_hbm.at[pl.ds(0, FFN_FIRST_ROWS)], o_ref, sem)
        copy.start()
        copy.wait()
        for r in range(0, FFN_FIRST_ROWS, NORM_ROWS):
            h_ref[r:r + NORM_ROWS, :] = _rms(o_ref[r:r + NORM_ROWS, :], g_ref[...]).astype(BF16)

    w13 = jnp.concatenate([w1_ref[...].astype(BF16), w3_ref[...].astype(BF16)], axis=1)
    w2 = w2_ref[...].astype(BF16)
    w13b_ref[...] = w13
    w2b_ref[...] = w2
    o_ref[...] += _swiglu(h_ref[...], w13, w2)

    if final_norm:
        @pl.when(j == pl.num_programs(0) - 1)
        def _():
            for r in range(0, FFN_FIRST_ROWS, NORM_ROWS):
                o_ref[r:r + NORM_ROWS, :] = _rms(o_ref[r:r + NORM_ROWS, :], fg_ref[...])


def _ffn_rest_kernel(x_ref, g_ref, w13_ref, w2_ref, fg_ref, y0_hbm, o_ref, h_ref, sem,
                     *, nf, n_first, final_norm):
    s = pl.program_id(0)
    j = lax.rem(jnp.maximum(s - n_first, 0), nf)

    @pl.when(s < n_first)
    def _():
        copy = pltpu.make_async_copy(y0_hbm.at[pl.ds(pl.multiple_of(s * FFN_TM, FFN_TM), FFN_TM)], o_ref, sem)
        copy.start()
        copy.wait()

    @pl.when(s >= n_first)
    def _():
        @pl.when(j == 0)
        def _():
            x = x_ref[...]
            h_ref[...] = _rms(x, g_ref[...]).astype(BF16)
            o_ref[...] = x

        o_ref[...] += _swiglu(h_ref[...], w13_ref[...], w2_ref[...])

        if final_norm:
            @pl.when(j == nf - 1)
            def _():
                o_ref[...] = _rms(o_ref[...], fg_ref[...])


def _ffn(x, gain, w13, w2, layer, final_gain=None):
    m, d = x.shape
    f = w2.shape[1]
    final_norm = final_gain is not None
    if final_gain is None:
        final_gain = gain
    vec = pl.BlockSpec((1, d), lambda s: (0, 0))

    nf = f // FFN_TF
    y0, w13b, w2b = pl.pallas_call(
        functools.partial(_ffn_first_kernel, final_norm=final_norm),
        out_shape=[jax.ShapeDtypeStruct((FFN_FIRST_ROWS, d), F32),
                   jax.ShapeDtypeStruct((nf, d, 2 * FFN_TF), BF16),
                   jax.ShapeDtypeStruct((f, d), BF16)],
        grid=(nf,),
        in_specs=[
            _HBM,
            vec,
            pl.BlockSpec((None, d, FFN_TF), lambda j: (layer, 0, j)),
            pl.BlockSpec((None, d, FFN_TF), lambda j: (layer, 0, j + nf)),
            pl.BlockSpec((None, FFN_TF, d), lambda j: (layer, j, 0)),
            vec,
        ],
        out_specs=[
            pl.BlockSpec((FFN_FIRST_ROWS, d), lambda j: (0, 0)),
            pl.BlockSpec((None, d, 2 * FFN_TF), lambda j: (j, 0, 0)),
            pl.BlockSpec((FFN_TF, d), lambda j: (j, 0)),
        ],
        scratch_shapes=[pltpu.VMEM((FFN_FIRST_ROWS, d), BF16), pltpu.SemaphoreType.DMA(())],
        compiler_params=_params(("arbitrary",)),
        name="ffn_first",
    )(x, gain, w13, w13, w2, final_gain)

    n_first = FFN_FIRST_ROWS // FFN_TM
    n_rest = m // FFN_TM - n_first
    rest = lambda s: jnp.maximum(s - n_first, 0)
    out_row = lambda s: (jnp.where(s < n_first, s, n_first + rest(s) // nf), 0)
    x_row = lambda s: (n_first + rest(s) // nf, 0)
    w_col = lambda s: lax.rem(rest(s), nf)
    return pl.pallas_call(
        functools.partial(_ffn_rest_kernel, nf=nf, n_first=n_first, final_norm=final_norm),
        out_shape=jax.ShapeDtypeStruct((m, d), F32),
        grid=(n_first + n_rest * nf,),
        in_specs=[
            pl.BlockSpec((FFN_TM, d), x_row),
            vec,
            pl.BlockSpec((None, d, 2 * FFN_TF), lambda s: (w_col(s), 0, 0)),
            pl.BlockSpec((FFN_TF, d), lambda s: (w_col(s), 0)),
            vec,
            _HBM,
        ],
        out_specs=pl.BlockSpec((FFN_TM, d), out_row),
        scratch_shapes=[pltpu.VMEM((FFN_TM, d), BF16), pltpu.SemaphoreType.DMA(())],
        compiler_params=_params(("arbitrary",)),
        name="ffn_rest",
    )(x, gain, w13b, w2b, final_gain, y0)


def _gmlp_kernel(x_ref, g_ref, win_hbm, lng_ref, lnb_ref, ws_ref, bs_ref, wout_hbm, o_ref,
                 win_ref, wout_ref, u_ref, sem, v_ref, uf_ref, *, layer):
    e = GMLP_WIDTH
    half = GMLP_TM // 2

    @pl.when(pl.program_id(0) == 0)
    def _():
        _load_weight(win_hbm.at[layer], win_ref, u_ref, sem, col0=0)
        _load_weight(win_hbm.at[layer], win_ref, u_ref, sem, col0=e)
        _load_weight(wout_hbm.at[layer], wout_ref, u_ref, sem)

    x = x_ref[...]
    h = _rms(x, g_ref[...]).astype(BF16)

    def gelu(z):
        return 0.5 * z * (1.0 + lax.erf(z * (2.0 ** -0.5)))

    v = gelu(_dot(h, win_ref[:, e:]))
    u = gelu(_dot(h, win_ref[:, :e]))
    u_ref[0] = u[:half]
    u_ref[1] = u[half:]
    mu = jnp.mean(v, axis=-1, keepdims=True)
    vc = v - mu
    var = jnp.mean(vc * vc, axis=-1, keepdims=True)
    v_ref[...] = (vc * lax.rsqrt(var + LN_EPS) * lng_ref[...] + lnb_ref[...]).astype(BF16)

    t_idx = lax.broadcasted_iota(jnp.int32, (CHUNK, CHUNK), 0)
    s_idx = lax.broadcasted_iota(jnp.int32, (CHUNK, CHUNK), 1)
    causal = t_idx >= s_idx
    for g in range(GMLP_GROUPS):
        w = jnp.where(causal, ws_ref[g], 0.0).astype(BF16)
        cols = slice(g * GMLP_GROUP_DIM, (g + 1) * GMLP_GROUP_DIM)
        for c in range(GMLP_TM // CHUNK):
            rows = slice(c * CHUNK, (c + 1) * CHUNK)
            r0 = (c * CHUNK) % half
            f = _dot(w, v_ref[rows, cols]) + bs_ref[g]
            uf_ref[rows, cols] = (u_ref[(c * CHUNK) // half, r0:r0 + CHUNK, cols] * f).astype(BF16)
    o_ref[...] = x + _dot(uf_ref[...], wout_ref[...])


def _gmlp(x, gain, w_in, ln_g, ln_b, w_s, b_s, w_out, layer):
    m, d = x.shape
    e = GMLP_WIDTH
    row = lambda i: (i, 0)
    bias = jnp.broadcast_to(b_s[:, :, None], (GMLP_GROUPS, CHUNK, GMLP_GROUP_DIM))
    assert d == e, "one staging width serves both weights"
    return pl.pallas_call(
        functools.partial(_gmlp_kernel, layer=layer),
        out_shape=jax.ShapeDtypeStruct((m, d), F32),
        grid=(m // GMLP_TM,),
        in_specs=[
            pl.BlockSpec((GMLP_TM, d), row),
            _resident((1, d)),
            _HBM,
            _resident((1, e)),
            _resident((1, e)),
            _resident((GMLP_GROUPS, CHUNK, CHUNK)),
            _resident((GMLP_GROUPS, CHUNK, GMLP_GROUP_DIM)),
            _HBM,
        ],
        out_specs=pl.BlockSpec((GMLP_TM, d), row),
        scratch_shapes=[pltpu.VMEM((d, 2 * e), BF16),
                        pltpu.VMEM((e, d), BF16),
                        pltpu.VMEM((2, GMLP_TM // 2, e), F32),
                        pltpu.SemaphoreType.DMA((2,)),
                        pltpu.VMEM((GMLP_TM, e), BF16),
                        pltpu.VMEM((GMLP_TM, e), BF16)],
        compiler_params=_params(("arbitrary",)),
        name="gmlp",
    )(x, gain, w_in, ln_g, ln_b, w_s, bias, w_out)


def _conv_kernel(x_ref, g_ref, win_hbm, cw_ref, wout_hbm, o_ref,
                 win_ref, win_stage, win_sem, wout_ref, wout_stage, wout_sem, z_ref, gb_ref, *, layer):
    d = D_MODEL
    tm = MIX_TM
    i = pl.program_id(0)

    @pl.when(i == 0)
    def _():
        _load_weight(win_hbm.at[layer], win_ref, win_stage, win_sem)
        _load_weight(wout_hbm.at[layer], wout_ref, wout_stage, wout_sem)
        z_ref[0:SUBLANES, :] = jnp.zeros((SUBLANES, d), F32)

    @pl.when(i > 0)
    def _():
        z_ref[0:SUBLANES, :] = z_ref[tm:tm + SUBLANES, :]

    x = x_ref[...]
    h = _rms(x, g_ref[...]).astype(BF16)
    for jb in range(d // CONV_TN):
        lo = jb * CONV_TN
        cols = slice(lo, lo + CONV_TN)
        gate_b = _dot(h, win_ref[:, lo:lo + CONV_TN])
        gate_c = _dot(h, win_ref[:, d + lo:d + lo + CONV_TN])
        val = _dot(h, win_ref[:, 2 * d + lo:2 * d + lo + CONV_TN])
        z = gate_c * val
        z_ref[SUBLANES:SUBLANES + tm, cols] = z
        conv = (cw_ref[0:1, cols] * z_ref[SUBLANES - 2:SUBLANES - 2 + tm, cols]
                + cw_ref[1:2, cols] * z_ref[SUBLANES - 1:SUBLANES - 1 + tm, cols]
                + cw_ref[2:3, cols] * z)
        gb_ref[:, cols] = (gate_b * conv).astype(BF16)
    o_ref[...] = x + _dot(gb_ref[...], wout_ref[...])


def _short_conv(x, gain, w_in, conv_w, w_out, layer):
    m, d = x.shape
    row = lambda i: (i, 0)
    return pl.pallas_call(
        functools.partial(_conv_kernel, layer=layer),
        out_shape=jax.ShapeDtypeStruct((m, d), F32),
        grid=(m // MIX_TM,),
        in_specs=[
            pl.BlockSpec((MIX_TM, d), row),
            _resident((1, d)),
            _HBM,
            _resident((CONV_WIDTH, d)),
            _HBM,
        ],
        out_specs=pl.BlockSpec((MIX_TM, d), row),
        scratch_shapes=[*_stage_scratch((d, 3 * d)),
                        *_stage_scratch((d, d)),
                        pltpu.VMEM((MIX_TM + SUBLANES, d), F32),
                        pltpu.VMEM((MIX_TM, d), BF16)],
        compiler_params=_params(("arbitrary",)),
        name="short_conv",
    )(x, gain, w_in, conv_w, w_out)


def _kv_kernel(mem_ref, g_ref, wkv_ref, kv_ref):
    mem_n = _rms(mem_ref[...], g_ref[...]).astype(BF16)
    kv_ref[...] = _dot(mem_n, wkv_ref[...].astype(BF16)).astype(BF16)


def _kv_proj(mem, gain, wkv, layer):
    mlen, d = mem.shape
    n = wkv.shape[2]
    return pl.pallas_call(
        _kv_kernel,
        out_shape=jax.ShapeDtypeStruct((mlen, n), BF16),
        grid=(n // KV_TN,),
        in_specs=[
            pl.BlockSpec((mlen, d), lambda j: (0, 0)),
            pl.BlockSpec((1, d), lambda j: (0, 0)),
            pl.BlockSpec((None, d, KV_TN), lambda j: (layer, 0, j)),
        ],
        out_specs=pl.BlockSpec((mlen, KV_TN), lambda j: (0, j)),
        compiler_params=_params(("parallel",)),
        name="kv_proj",
    )(mem, gain, wkv)


def _xattn_kernel(x_ref, xn_ref, g_ref, wq_hbm, k_ref, v_ref, wo_hbm, o_ref,
                  wq_ref, wo_ref, stage, sem, qa_ref, qb_ref, oa_ref, ob_ref, *, layer):
    hd = XATTN_HEAD_DIM
    tm = ATT_TM

    def normed(x):
        return _rms(x, g_ref[...]).astype(BF16)

    def attend(q_ref, x, ob_scr, h_next, q_next):
        for hh in range(XATTN_HEADS):
            cols = slice(hh * hd, (hh + 1) * hd)
            s = lax.dot_general(q_ref[:, cols], k_ref[:, cols], (((1,), (1,)), ((), ())),
                                preferred_element_type=F32) * (hd ** -0.5)
            q_next[:, cols] = _dot(h_next, wq_ref[:, cols]).astype(BF16)
            p = jnp.exp(s - jnp.max(s, axis=-1, keepdims=True))
            p = p / jnp.sum(p, axis=-1, keepdims=True)
            ob_scr[:, cols] = _dot(p.astype(BF16), v_ref[:, cols]).astype(BF16)
        return x + _dot(ob_scr[...], wo_ref[...])

    @pl.when(pl.program_id(0) == 0)
    def _():
        _load_weight(wq_hbm.at[layer], wq_ref, stage, sem)
        _load_weight(wo_hbm.at[layer], wo_ref, stage, sem)
        qa_ref[...] = _dot(normed(x_ref[0:tm, :]), wq_ref[...]).astype(BF16)

    x0 = x_ref[0:tm, :]
    x1 = x_ref[tm:2 * tm, :]
    o_ref[0:tm, :] = attend(qa_ref, x0, oa_ref, normed(x1), qb_ref)
    o_ref[tm:2 * tm, :] = attend(qb_ref, x1, ob_ref, normed(xn_ref[...]), qa_ref)


def _xattn(x, gain, wq, kv, wo, layer):
    m, d = x.shape
    mlen = kv.shape[0]
    n = m // (2 * ATT_TM)
    row = lambda i: (i, 0)
    w_scratch, stage, sem = _stage_scratch((d, d))
    half = pltpu.VMEM((ATT_TM, d), BF16)
    return pl.pallas_call(
        functools.partial(_xattn_kernel, layer=layer),
        out_shape=jax.ShapeDtypeStruct((m, d), F32),
        grid=(n,),
        in_specs=[
            pl.BlockSpec((2 * ATT_TM, d), row),
            pl.BlockSpec((ATT_TM, d), lambda i: (jnp.minimum(2 * i + 2, 2 * n - 1), 0)),
            _resident((1, d)),
            _HBM,
            pl.BlockSpec((mlen, d), lambda i: (0, 0), pipeline_mode=pl.Buffered(1)),
            pl.BlockSpec((mlen, d), lambda i: (0, 1), pipeline_mode=pl.Buffered(1)),
            _HBM,
        ],
        out_specs=pl.BlockSpec((2 * ATT_TM, d), row),
        scratch_shapes=[w_scratch, w_scratch, stage, sem, half, half, half, half],
        compiler_params=_params(("arbitrary",)),
        name="xattn",
    )(x, x, gain, wq, kv, kv, wo)


def kernel(x, mem, ffn1_norm, ffn1_w13, ffn1_w2, mix_norm, gmlp_w_in, gmlp_ln_g, gmlp_ln_b, gmlp_w_s, gmlp_b_s, gmlp_w_out, conv_w_in, conv_w, conv_w_out, xattn_norm, mem_norm, xattn_wq, xattn_wkv, xattn_wo, ffn2_norm, ffn2_w13, ffn2_w2, final_norm):
    bsz, seq, d = x.shape
    assert (bsz, seq, d) == (1, SEQ, D_MODEL), "conv carry assumes one sequence laid out along rows"
    xs = x.reshape(seq, d)
    mem2 = mem.reshape(MEM_LEN, d)
    vec = lambda v: v.reshape(1, -1)

    for i in range(DEPTH):
        xs = _ffn(xs, vec(ffn1_norm[i]), ffn1_w13, ffn1_w2, i)
        j = i // 2
        if i % 2 == 0:
            xs = _gmlp(xs, vec(mix_norm[i]), gmlp_w_in, vec(gmlp_ln_g[j]), vec(gmlp_ln_b[j]),
                       gmlp_w_s[j], gmlp_b_s[j], gmlp_w_out, j)
        else:
            xs = _short_conv(xs, vec(mix_norm[i]), conv_w_in, conv_w[j], conv_w_out, j)
        kv = _kv_proj(mem2, vec(mem_norm[i]), xattn_wkv, i)
        xs = _xattn(xs, vec(xattn_norm[i]), xattn_wq, kv, xattn_wo, i)
        last = i == DEPTH - 1
        xs = _ffn(xs, vec(ffn2_norm[i]), ffn2_w13, ffn2_w2, i,
                  final_gain=vec(final_norm) if last else None)
    return xs.reshape(bsz, seq, d)
```

```python
import functools

import jax
import jax.numpy as jnp
from jax import lax
from jax.experimental import pallas as pl
from jax.experimental.pallas import tpu as pltpu

D_MODEL = 2048
SEQ = 8192
DEPTH = 2
MEM_LEN = 256
D_FF = 5632
CHUNK = 128
GMLP_WIDTH = 2048
GMLP_GROUPS = 8
GMLP_GROUP_DIM = GMLP_WIDTH // GMLP_GROUPS
CONV_WIDTH = 3
XATTN_HEADS = 4
XATTN_HEAD_DIM = D_MODEL // XATTN_HEADS
RMS_EPS = 1e-6
LN_EPS = 1e-5

F32 = jnp.float32
BF16 = jnp.bfloat16

V7X_VMEM_LIMIT_BYTES = 60 * 1024 * 1024
SUBLANES = 8

FFN_TM = 1024
FFN_TF = 512
FFN_FIRST_ROWS = FFN_TM
NORM_ROWS = 256
MIX_TM = 256
GMLP_TM = 512
ATT_TM = 256
CONV_TN = 512
KV_TN = 512
STAGE_BYTES = 2 * 1024 * 1024


def _rms(x, g):
    return x * lax.rsqrt(jnp.mean(x * x, axis=-1, keepdims=True) + RMS_EPS) * g


def _dot(a, b):
    return jnp.dot(a, b, preferred_element_type=F32)


def _params(semantics):
    return pltpu.CompilerParams(dimension_semantics=semantics,
                                vmem_limit_bytes=V7X_VMEM_LIMIT_BYTES)


def _resident(shape):
    return pl.BlockSpec(shape, lambda *_: (0,) * len(shape), pipeline_mode=pl.Buffered(1))


_HBM = pl.BlockSpec(memory_space=pl.ANY)


def _stage_rows(w_shape):
    rows = STAGE_BYTES // (w_shape[1] * 4) // 16 * 16
    while w_shape[0] % rows:
        rows -= 16
    return rows


def _stage_scratch(w_shape):
    return [pltpu.VMEM(w_shape, BF16),
            pltpu.VMEM((2, _stage_rows(w_shape), w_shape[1]), F32),
            pltpu.SemaphoreType.DMA((2,))]


def _load_weight(w_hbm, w_vmem, stage, sem, col0=0):
    rows, width = stage.shape[1], stage.shape[2]
    n = w_hbm.shape[0] // rows
    cols = slice(col0, col0 + width)

    def copy(c, slot):
        return pltpu.make_async_copy(w_hbm.at[pl.ds(c * rows, rows), cols], stage.at[slot], sem.at[slot])

    copy(0, 0).start()

    def body(c, carry):
        slot = lax.rem(c, 2)

        @pl.when(c + 1 < n)
        def _():
            copy(c + 1, 1 - slot).start()

        copy(c, slot).wait()
        w_vmem[pl.ds(pl.multiple_of(c * rows, rows), rows), cols] = stage[slot].astype(BF16)
        return carry

    lax.fori_loop(0, n, body, 0)


def _swiglu(h, w13, w2):
    tf = w13.shape[1] // 2
    gu = _dot(h, w13)
    gate = gu[:, :tf]
    up = gu[:, tf:]
    act = (gate * (0.5 / (1.0 + jnp.exp(-gate))) * up).astype(BF16)
    return _dot(act, w2)


def _ffn_first_kernel(x_hbm, g_ref, w1_ref, w3_ref, w2_ref, fg_ref, o_ref, w13b_ref, w2b_ref,
                      h_ref, sem, *, final_norm):
    j = pl.program_id(0)

    @pl.when(j == 0)
    def _():
        copy = pltpu.make_async_copy(x_hbm.at[pl.ds(0, FFN_FIRST_ROWS)], o_ref, sem)
        copy.start()
        copy.wait()
        for r in range(0, FFN_FIRST_ROWS, NORM_ROWS):
            h_ref[r:r + NORM_ROWS, :] = _rms(o_ref[r:r + NORM_ROWS, :], g_ref[...]).astype(BF16)

    w13 = jnp.concatenate([w1_ref[...].astype(BF16), w3_ref[...].astype(BF16)], axis=1)
    w2 = w2_ref[...].astype(BF16)
    w13b_ref[...] = w13
    w2b_ref[...] = w2
    o_ref[...] += _swiglu(h_ref[...], w13, w2)

    if final_norm:
        @pl.when(j == pl.num_programs(0) - 1)
        def _():
            for r in range(0, FFN_FIRST_ROWS, NORM_ROWS):
                o_ref[r:r + NORM_ROWS, :] = _rms(o_ref[r:r + NORM_ROWS, :], fg_ref[...])


def _ffn_rest_kernel(x_hbm, g_ref, w13_ref, w2_ref, fg_ref, y0_hbm, o_ref, h_ref, xs_ref, sem, xsem,
                     *, nf, n_first, n_blocks, final_norm):
    tf = FFN_TF
    n_pair = nf // 2
    steps = n_pair + nf % 2
    s = pl.program_id(0)
    t = jnp.maximum(s - n_first, 0)
    j = lax.rem(t, steps)
    blk = n_first + t // steps

    def x_copy(b):
        return pltpu.make_async_copy(x_hbm.at[pl.ds(pl.multiple_of(b * FFN_TM, FFN_TM), FFN_TM)], xs_ref, xsem)

    @pl.when(s < n_first)
    def _():
        @pl.when(s == 0)
        def _():
            x_copy(n_first).start()

        copy = pltpu.make_async_copy(y0_hbm.at[pl.ds(pl.multiple_of(s * FFN_TM, FFN_TM), FFN_TM)], o_ref, sem)
        copy.start()
        copy.wait()

    @pl.when(s >= n_first)
    def _():
        @pl.when(j == 0)
        def _():
            x_copy(blk).wait()
            x = xs_ref[...]
            h_ref[...] = _rms(x, g_ref[...]).astype(BF16)
            o_ref[...] = x

            @pl.when(blk + 1 < n_blocks)
            def _():
                x_copy(blk + 1).start()

        @pl.when(j < n_pair)
        def _():
            o_ref[...] += _swiglu(h_ref[...], w13_ref[0], w2_ref[0:tf, :])
            o_ref[...] += _swiglu(h_ref[...], w13_ref[1], w2_ref[tf:2 * tf, :])

        if nf % 2:
            @pl.when(j == n_pair)
            def _():
                o_ref[...] += _swiglu(h_ref[...], w13_ref[0], w2_ref[0:tf, :])

        if final_norm:
            @pl.when(j == steps - 1)
            def _():
                o_ref[...] = _rms(o_ref[...], fg_ref[...])


def _ffn(x, gain, w13, w2, layer, final_gain=None):
    m, d = x.shape
    f = w2.shape[1]
    final_norm = final_gain is not None
    if final_gain is None:
        final_gain = gain
    vec = pl.BlockSpec((1, d), lambda s: (0, 0))

    nf = f // FFN_TF
    y0, w13b, w2b = pl.pallas_call(
        functools.partial(_ffn_first_kernel, final_norm=final_norm),
        out_shape=[jax.ShapeDtypeStruct((FFN_FIRST_ROWS, d), F32),
                   jax.ShapeDtypeStruct((nf, d, 2 * FFN_TF), BF16),
                   jax.ShapeDtypeStruct((f, d), BF16)],
        grid=(nf,),
        in_specs=[
            _HBM,
            vec,
            pl.BlockSpec((None, d, FFN_TF), lambda j: (layer, 0, j)),
            pl.BlockSpec((None, d, FFN_TF), lambda j: (layer, 0, j + nf)),
            pl.BlockSpec((None, FFN_TF, d), lambda j: (layer, j, 0)),
            vec,
        ],
        out_specs=[
            pl.BlockSpec((FFN_FIRST_ROWS, d), lambda j: (0, 0)),
            pl.BlockSpec((None, d, 2 * FFN_TF), lambda j: (j, 0, 0)),
            pl.BlockSpec((FFN_TF, d), lambda j: (j, 0)),
        ],
        scratch_shapes=[pltpu.VMEM((FFN_FIRST_ROWS, d), BF16), pltpu.SemaphoreType.DMA(())],
        compiler_params=_params(("arbitrary",)),
        name="ffn_first",
    )(x, gain, w13, w13, w2, final_gain)

    n_first = FFN_FIRST_ROWS // FFN_TM
    n_blocks = m // FFN_TM
    steps = nf // 2 + nf % 2
    assert n_first == 1
    rest = lambda s: jnp.maximum(s - n_first, 0)
    out_row = lambda s: (jnp.where(s < n_first, s, n_first + rest(s) // steps), 0)
    w_pair = lambda s: lax.rem(rest(s), steps)
    return pl.pallas_call(
        functools.partial(_ffn_rest_kernel, nf=nf, n_first=n_first, n_blocks=n_blocks, final_norm=final_norm),
        out_shape=jax.ShapeDtypeStruct((m, d), F32),
        grid=(n_first + (n_blocks - n_first) * steps,),
        in_specs=[
            _HBM,
            vec,
            pl.BlockSpec((2, d, 2 * FFN_TF), lambda s: (w_pair(s), 0, 0)),
            pl.BlockSpec((2 * FFN_TF, d), lambda s: (w_pair(s), 0)),
            vec,
            _HBM,
        ],
        out_specs=pl.BlockSpec((FFN_TM, d), out_row),
        scratch_shapes=[pltpu.VMEM((FFN_TM, d), BF16), pltpu.VMEM((FFN_TM, d), F32),
                        pltpu.SemaphoreType.DMA(()), pltpu.SemaphoreType.DMA(())],
        compiler_params=_params(("arbitrary",)),
        name="ffn_rest",
    )(x, gain, w13b, w2b, final_gain, y0)


def _gmlp_kernel(x_ref, g_ref, win_hbm, lng_ref, lnb_ref, ws_ref, bs_ref, wout_hbm, o_ref,
                 win_ref, wout_ref, u_ref, sem, v_ref, uf_ref, *, layer):
    e = GMLP_WIDTH
    half = GMLP_TM // 2

    @pl.when(pl.program_id(0) == 0)
    def _():
        _load_weight(win_hbm.at[layer], win_ref, u_ref, sem, col0=0)
        _load_weight(win_hbm.at[layer], win_ref, u_ref, sem, col0=e)
        _load_weight(wout_hbm.at[layer], wout_ref, u_ref, sem)

    x = x_ref[...]
    h = _rms(x, g_ref[...]).astype(BF16)

    def gelu(z):
        return 0.5 * z * (1.0 + lax.erf(z * (2.0 ** -0.5)))

    v = gelu(_dot(h, win_ref[:, e:]))
    u = gelu(_dot(h, win_ref[:, :e]))
    u_ref[0] = u[:half]
    u_ref[1] = u[half:]
    mu = jnp.mean(v, axis=-1, keepdims=True)
    vc = v - mu
    var = jnp.mean(vc * vc, axis=-1, keepdims=True)
    v_ref[...] = (vc * lax.rsqrt(var + LN_EPS) * lng_ref[...] + lnb_ref[...]).astype(BF16)

    t_idx = lax.broadcasted_iota(jnp.int32, (CHUNK, CHUNK), 0)
    s_idx = lax.broadcasted_iota(jnp.int32, (CHUNK, CHUNK), 1)
    causal = t_idx >= s_idx
    for g in range(GMLP_GROUPS):
        w = jnp.where(causal, ws_ref[g], 0.0).astype(BF16)
        cols = slice(g * GMLP_GROUP_DIM, (g + 1) * GMLP_GROUP_DIM)
        for c in range(GMLP_TM // CHUNK):
            rows = slice(c * CHUNK, (c + 1) * CHUNK)
            r0 = (c * CHUNK) % half
            f = _dot(w, v_ref[rows, cols]) + bs_ref[g]
            uf_ref[rows, cols] = (u_ref[(c * CHUNK) // half, r0:r0 + CHUNK, cols] * f).astype(BF16)
    o_ref[...] = x + _dot(uf_ref[...], wout_ref[...])


def _gmlp(x, gain, w_in, ln_g, ln_b, w_s, b_s, w_out, layer):
    m, d = x.shape
    e = GMLP_WIDTH
    row = lambda i: (i, 0)
    bias = jnp.broadcast_to(b_s[:, :, None], (GMLP_GROUPS, CHUNK, GMLP_GROUP_DIM))
    assert d == e, "one staging width serves both weights"
    return pl.pallas_call(
        functools.partial(_gmlp_kernel, layer=layer),
        out_shape=jax.ShapeDtypeStruct((m, d), F32),
        grid=(m // GMLP_TM,),
        in_specs=[
            pl.BlockSpec((GMLP_TM, d), row),
            _resident((1, d)),
            _HBM,
            _resident((1, e)),
            _resident((1, e)),
            _resident((GMLP_GROUPS, CHUNK, CHUNK)),
            _resident((GMLP_GROUPS, CHUNK, GMLP_GROUP_DIM)),
            _HBM,
        ],
        out_specs=pl.BlockSpec((GMLP_TM, d), row),
        scratch_shapes=[pltpu.VMEM((d, 2 * e), BF16),
                        pltpu.VMEM((e, d), BF16),
                        pltpu.VMEM((2, GMLP_TM // 2, e), F32),
                        pltpu.SemaphoreType.DMA((2,)),
                        pltpu.VMEM((GMLP_TM, e), BF16),
                        pltpu.VMEM((GMLP_TM, e), BF16)],
        compiler_params=_params(("arbitrary",)),
        name="gmlp",
    )(x, gain, w_in, ln_g, ln_b, w_s, bias, w_out)


def _conv_kernel(x_ref, g_ref, win_hbm, cw_ref, wout_hbm, o_ref,
                 win_ref, win_stage, win_sem, wout_ref, wout_stage, wout_sem, z_ref, gb_ref, *, layer):
    d = D_MODEL
    tm = MIX_TM
    i = pl.program_id(0)

    @pl.when(i == 0)
    def _():
        _load_weight(win_hbm.at[layer], win_ref, win_stage, win_sem)
        _load_weight(wout_hbm.at[layer], wout_ref, wout_stage, wout_sem)
        z_ref[0:SUBLANES, :] = jnp.zeros((SUBLANES, d), F32)

    @pl.when(i > 0)
    def _():
        z_ref[0:SUBLANES, :] = z_ref[tm:tm + SUBLANES, :]

    x = x_ref[...]
    h = _rms(x, g_ref[...]).astype(BF16)
    for jb in range(d // CONV_TN):
        lo = jb * CONV_TN
        cols = slice(lo, lo + CONV_TN)
        gate_b = _dot(h, win_ref[:, lo:lo + CONV_TN])
        gate_c = _dot(h, win_ref[:, d + lo:d + lo + CONV_TN])
        val = _dot(h, win_ref[:, 2 * d + lo:2 * d + lo + CONV_TN])
        z = gate_c * val
        z_ref[SUBLANES:SUBLANES + tm, cols] = z
        conv = (cw_ref[0:1, cols] * z_ref[SUBLANES - 2:SUBLANES - 2 + tm, cols]
                + cw_ref[1:2, cols] * z_ref[SUBLANES - 1:SUBLANES - 1 + tm, cols]
                + cw_ref[2:3, cols] * z)
        gb_ref[:, cols] = (gate_b * conv).astype(BF16)
    o_ref[...] = x + _dot(gb_ref[...], wout_ref[...])


def _short_conv(x, gain, w_in, conv_w, w_out, layer):
    m, d = x.shape
    row = lambda i: (i, 0)
    return pl.pallas_call(
        functools.partial(_conv_kernel, layer=layer),
        out_shape=jax.ShapeDtypeStruct((m, d), F32),
        grid=(m // MIX_TM,),
        in_specs=[
            pl.BlockSpec((MIX_TM, d), row),
            _resident((1, d)),
            _HBM,
            _resident((CONV_WIDTH, d)),
            _HBM,
        ],
        out_specs=pl.BlockSpec((MIX_TM, d), row),
        scratch_shapes=[*_stage_scratch((d, 3 * d)),
                        *_stage_scratch((d, d)),
                        pltpu.VMEM((MIX_TM + SUBLANES, d), F32),
                        pltpu.VMEM((MIX_TM, d), BF16)],
        compiler_params=_params(("arbitrary",)),
        name="short_conv",
    )(x, gain, w_in, conv_w, w_out)


def _kv_kernel(mem_ref, g_ref, wkv_ref, kv_ref):
    mem_n = _rms(mem_ref[...], g_ref[...]).astype(BF16)
    kv_ref[...] = _dot(mem_n, wkv_ref[...].astype(BF16)).astype(BF16)


def _kv_proj(mem, gain, wkv, layer):
    mlen, d = mem.shape
    n = wkv.shape[2]
    return pl.pallas_call(
        _kv_kernel,
        out_shape=jax.ShapeDtypeStruct((mlen, n), BF16),
        grid=(n // KV_TN,),
        in_specs=[
            pl.BlockSpec((mlen, d), lambda j: (0, 0)),
            pl.BlockSpec((1, d), lambda j: (0, 0)),
            pl.BlockSpec((None, d, KV_TN), lambda j: (layer, 0, j)),
        ],
        out_specs=pl.BlockSpec((mlen, KV_TN), lambda j: (0, j)),
        compiler_params=_params(("parallel",)),
        name="kv_proj",
    )(mem, gain, wkv)


def _xattn_kernel(x_ref, xn_ref, g_ref, wq_hbm, k_ref, v_ref, wo_hbm, o_ref,
                  wq_ref, wo_ref, stage, sem, qa_ref, qb_ref, oa_ref, ob_ref, *, layer):
    hd = XATTN_HEAD_DIM
    tm = ATT_TM

    def normed(x):
        return _rms(x, g_ref[...]).astype(BF16)

    def attend(q_ref, x, ob_scr, h_next, q_next):
        for hh in range(XATTN_HEADS):
            cols = slice(hh * hd, (hh + 1) * hd)
            s = lax.dot_general(q_ref[:, cols], k_ref[:, cols], (((1,), (1,)), ((), ())),
                                preferred_element_type=F32) * (hd ** -0.5)
            q_next[:, cols] = _dot(h_next, wq_ref[:, cols]).astype(BF16)
            p = jnp.exp(s - jnp.max(s, axis=-1, keepdims=True))
            p = p / jnp.sum(p, axis=-1, keepdims=True)
            ob_scr[:, cols] = _dot(p.astype(BF16), v_ref[:, cols]).astype(BF16)
        return x + _dot(ob_scr[...], wo_ref[...])

    @pl.when(pl.program_id(0) == 0)
    def _():
        _load_weight(wq_hbm.at[layer], wq_ref, stage, sem)
        _load_weight(wo_hbm.at[layer], wo_ref, stage, sem)
        qa_ref[...] = _dot(normed(x_ref[0:tm, :]), wq_ref[...]).astype(BF16)

    x0 = x_ref[0:tm, :]
    x1 = x_ref[tm:2 * tm, :]
    o_ref[0:tm, :] = attend(qa_ref, x0, oa_ref, normed(x1), qb_ref)
    o_ref[tm:2 * tm, :] = attend(qb_ref, x1, ob_ref, normed(xn_ref[...]), qa_ref)


def _xattn(x, gain, wq, kv, wo, layer):
    m, d = x.shape
    mlen = kv.shape[0]
    n = m // (2 * ATT_TM)
    row = lambda i: (i, 0)
    w_scratch, stage, sem = _stage_scratch((d, d))
    half = pltpu.VMEM((ATT_TM, d), BF16)
    return pl.pallas_call(
        functools.partial(_xattn_kernel, layer=layer),
        out_shape=jax.ShapeDtypeStruct((m, d), F32),
        grid=(n,),
        in_specs=[
            pl.BlockSpec((2 * ATT_TM, d), row),
            pl.BlockSpec((ATT_TM, d), lambda i: (jnp.minimum(2 * i + 2, 2 * n - 1), 0)),
            _resident((1, d)),
            _HBM,
            pl.BlockSpec((mlen, d), lambda i: (0, 0), pipeline_mode=pl.Buffered(1)),
            pl.BlockSpec((mlen, d), lambda i: (0, 1), pipeline_mode=pl.Buffered(1)),
            _HBM,
        ],
        out_specs=pl.BlockSpec((2 * ATT_TM, d), row),
        scratch_shapes=[w_scratch, w_scratch, stage, sem, half, half, half, half],
        compiler_params=_params(("arbitrary",)),
        name="xattn",
    )(x, x, gain, wq, kv, kv, wo)


def kernel(x, mem, ffn1_norm, ffn1_w13, ffn1_w2, mix_norm, gmlp_w_in, gmlp_ln_g, gmlp_ln_b, gmlp_w_s, gmlp_b_s, gmlp_w_out, conv_w_in, conv_w, conv_w_out, xattn_norm, mem_norm, xattn_wq, xattn_wkv, xattn_wo, ffn2_norm, ffn2_w13, ffn2_w2, final_norm):
    bsz, seq, d = x.shape
    assert (bsz, seq, d) == (1, SEQ, D_MODEL), "conv carry assumes one sequence laid out along rows"
    xs = x.reshape(seq, d)
    mem2 = mem.reshape(MEM_LEN, d)
    vec = lambda v: v.reshape(1, -1)

    for i in range(DEPTH):
        xs = _ffn(xs, vec(ffn1_norm[i]), ffn1_w13, ffn1_w2, i)
        j = i // 2
        if i % 2 == 0:
            xs = _gmlp(xs, vec(mix_norm[i]), gmlp_w_in, vec(gmlp_ln_g[j]), vec(gmlp_ln_b[j]),
                       gmlp_w_s[j], gmlp_b_s[j], gmlp_w_out, j)
        else:
            xs = _short_conv(xs, vec(mix_norm[i]), conv_w_in, conv_w[j], conv_w_out, j)
        kv = _kv_proj(mem2, vec(mem_norm[i]), xattn_wkv, i)
        xs = _xattn(xs, vec(xattn_norm[i]), xattn_wq, kv, xattn_wo, i)
        last = i == DEPTH - 1
        xs = _ffn(xs, vec(ffn2_norm[i]), ffn2_w13, ffn2_w2, i,
                  final_gain=vec(final_norm) if last else None)
    return xs.reshape(bsz, seq, d)
```

```python
import functools

import jax
import jax.numpy as jnp
from jax import lax
from jax.experimental import pallas as pl
from jax.experimental.pallas import tpu as pltpu

D_MODEL = 2048
SEQ = 8192
DEPTH = 2
MEM_LEN = 256
D_FF = 5632
CHUNK = 128
GMLP_WIDTH = 2048
GMLP_GROUPS = 8
GMLP_GROUP_DIM = GMLP_WIDTH // GMLP_GROUPS
CONV_WIDTH = 3
XATTN_HEADS = 4
XATTN_HEAD_DIM = D_MODEL // XATTN_HEADS
RMS_EPS = 1e-6
LN_EPS = 1e-5

F32 = jnp.float32
BF16 = jnp.bfloat16

V7X_VMEM_LIMIT_BYTES = 60 * 1024 * 1024
SUBLANES = 8

FFN_TM = 1024
FFN_TF = 512
FFN_FIRST_ROWS = FFN_TM
NORM_ROWS = 256
MIX_TM = 256
GMLP_TM = 512
ATT_TM = 256
CONV_TN = 512
KV_TN = 1024
STAGE_BYTES = 2 * 1024 * 1024


def _rms(x, g):
    return x * lax.rsqrt(jnp.mean(x * x, axis=-1, keepdims=True) + RMS_EPS) * g


def _dot(a, b):
    return jnp.dot(a, b, preferred_element_type=F32)


def _params(semantics):
    return pltpu.CompilerParams(dimension_semantics=semantics,
                                vmem_limit_bytes=V7X_VMEM_LIMIT_BYTES)


def _resident(shape):
    return pl.BlockSpec(shape, lambda *_: (0,) * len(shape), pipeline_mode=pl.Buffered(1))


_HBM = pl.BlockSpec(memory_space=pl.ANY)


def _stage_rows(w_shape):
    rows = STAGE_BYTES // (w_shape[1] * 4) // 16 * 16
    while w_shape[0] % rows:
        rows -= 16
    return rows


def _stage_scratch(w_shape):
    return [pltpu.VMEM(w_shape, BF16),
            pltpu.VMEM((2, _stage_rows(w_shape), w_shape[1]), F32),
            pltpu.SemaphoreType.DMA((2,))]


def _load_weight(w_hbm, w_vmem, stage, sem, col0=0):
    rows, width = stage.shape[1], stage.shape[2]
    n = w_hbm.shape[0] // rows
    cols = slice(col0, col0 + width)

    def copy(c, slot):
        return pltpu.make_async_copy(w_hbm.at[pl.ds(c * rows, rows), cols], stage.at[slot], sem.at[slot])

    copy(0, 0).start()

    def body(c, carry):
        slot = lax.rem(c, 2)

        @pl.when(c + 1 < n)
        def _():
            copy(c + 1, 1 - slot).start()

        copy(c, slot).wait()
        w_vmem[pl.ds(pl.multiple_of(c * rows, rows), rows), cols] = stage[slot].astype(BF16)
        return carry

    lax.fori_loop(0, n, body, 0)


def _swiglu(h, w13, w2):
    tf = w13.shape[1] // 2
    gu = _dot(h, w13)
    gate = gu[:, :tf]
    up = gu[:, tf:]
    act = (gate * (0.5 / (1.0 + jnp.exp(-gate))) * up).astype(BF16)
    return _dot(act, w2)


def _ffn_first_kernel(x_hbm, g_ref, w1_ref, w3_ref, w2_ref, fg_ref, o_ref, w13b_ref, w2b_ref,
                      h_ref, sem, *, final_norm):
    j = pl.program_id(0)

    @pl.when(j == 0)
    def _():
        copy = pltpu.make_async_copy(x_hbm.at[pl.ds(0, FFN_FIRST_ROWS)], o_ref, sem)
        copy.start()
        copy.wait()
        for r in range(0, FFN_FIRST_ROWS, NORM_ROWS):
            h_ref[r:r + NORM_ROWS, :] = _rms(o_ref[r:r + NORM_ROWS, :], g_ref[...]).astype(BF16)

    w13 = jnp.concatenate([w1_ref[...].astype(BF16), w3_ref[...].astype(BF16)], axis=1)
    w2 = w2_ref[...].astype(BF16)
    w13b_ref[...] = w13
    w2b_ref[...] = w2
    o_ref[...] += _swiglu(h_ref[...], w13, w2)

    if final_norm:
        @pl.when(j == pl.num_programs(0) - 1)
        def _():
            for r in range(0, FFN_FIRST_ROWS, NORM_ROWS):
                o_ref[r:r + NORM_ROWS, :] = _rms(o_ref[r:r + NORM_ROWS, :], fg_ref[...])


def _ffn_rest_kernel(x_hbm, g_ref, w13_ref, w2_ref, fg_ref, y0_hbm, o_ref, h_ref, xs_ref, sem, xsem,
                     *, nf, n_first, n_blocks, final_norm):
    tf = FFN_TF
    n_pair = nf // 2
    steps = n_pair + nf % 2
    s = pl.program_id(0)
    t = jnp.maximum(s - n_first, 0)
    j = lax.rem(t, steps)
    blk = n_first + t // steps

    def x_copy(b):
        return pltpu.make_async_copy(x_hbm.at[pl.ds(pl.multiple_of(b * FFN_TM, FFN_TM), FFN_TM)], xs_ref, xsem)

    @pl.when(s < n_first)
    def _():
        @pl.when(s == 0)
        def _():
            x_copy(n_first).start()

        copy = pltpu.make_async_copy(y0_hbm.at[pl.ds(pl.multiple_of(s * FFN_TM, FFN_TM), FFN_TM)], o_ref, sem)
        copy.start()
        copy.wait()

    @pl.when(s >= n_first)
    def _():
        @pl.when(j == 0)
        def _():
            x_copy(blk).wait()
            h_ref[...] = _rms(xs_ref[...], g_ref[...]).astype(BF16)
            o_ref[...] = xs_ref[...] + _swiglu(h_ref[...], w13_ref[0], w2_ref[0:tf, :])
            o_ref[...] += _swiglu(h_ref[...], w13_ref[1], w2_ref[tf:2 * tf, :])

            @pl.when(blk + 1 < n_blocks)
            def _():
                x_copy(blk + 1).start()

        @pl.when(jnp.logical_and(j > 0, j < n_pair))
        def _():
            o_ref[...] += _swiglu(h_ref[...], w13_ref[0], w2_ref[0:tf, :])
            o_ref[...] += _swiglu(h_ref[...], w13_ref[1], w2_ref[tf:2 * tf, :])

        if nf % 2:
            @pl.when(j == n_pair)
            def _():
                o_ref[...] += _swiglu(h_ref[...], w13_ref[0], w2_ref[0:tf, :])

        if final_norm:
            @pl.when(j == steps - 1)
            def _():
                o_ref[...] = _rms(o_ref[...], fg_ref[...])


def _ffn(x, gain, w13, w2, layer, final_gain=None):
    m, d = x.shape
    f = w2.shape[1]
    final_norm = final_gain is not None
    if final_gain is None:
        final_gain = gain
    vec = pl.BlockSpec((1, d), lambda s: (0, 0))

    nf = f // FFN_TF
    y0, w13b, w2b = pl.pallas_call(
        functools.partial(_ffn_first_kernel, final_norm=final_norm),
        out_shape=[jax.ShapeDtypeStruct((FFN_FIRST_ROWS, d), F32),
                   jax.ShapeDtypeStruct((nf, d, 2 * FFN_TF), BF16),
                   jax.ShapeDtypeStruct((f, d), BF16)],
        grid=(nf,),
        in_specs=[
            _HBM,
            vec,
            pl.BlockSpec((None, d, FFN_TF), lambda j: (layer, 0, j)),
            pl.BlockSpec((None, d, FFN_TF), lambda j: (layer, 0, j + nf)),
            pl.BlockSpec((None, FFN_TF, d), lambda j: (layer, j, 0)),
            vec,
        ],
        out_specs=[
            pl.BlockSpec((FFN_FIRST_ROWS, d), lambda j: (0, 0)),
            pl.BlockSpec((None, d, 2 * FFN_TF), lambda j: (j, 0, 0)),
            pl.BlockSpec((FFN_TF, d), lambda j: (j, 0)),
        ],
        scratch_shapes=[pltpu.VMEM((FFN_FIRST_ROWS, d), BF16), pltpu.SemaphoreType.DMA(())],
        compiler_params=_params(("arbitrary",)),
        name="ffn_first",
    )(x, gain, w13, w13, w2, final_gain)

    n_first = FFN_FIRST_ROWS // FFN_TM
    n_blocks = m // FFN_TM
    steps = nf // 2 + nf % 2
    assert n_first == 1
    rest = lambda s: jnp.maximum(s - n_first, 0)
    out_row = lambda s: (jnp.where(s < n_first, s, n_first + rest(s) // steps), 0)
    w_pair = lambda s: lax.rem(rest(s), steps)
    return pl.pallas_call(
        functools.partial(_ffn_rest_kernel, nf=nf, n_first=n_first, n_blocks=n_blocks, final_norm=final_norm),
        out_shape=jax.ShapeDtypeStruct((m, d), F32),
        grid=(n_first + (n_blocks - n_first) * steps,),
        in_specs=[
            _HBM,
            vec,
            pl.BlockSpec((2, d, 2 * FFN_TF), lambda s: (w_pair(s), 0, 0)),
            pl.BlockSpec((2 * FFN_TF, d), lambda s: (w_pair(s), 0)),
            vec,
            _HBM,
        ],
        out_specs=pl.BlockSpec((FFN_TM, d), out_row),
        scratch_shapes=[pltpu.VMEM((FFN_TM, d), BF16), pltpu.VMEM((FFN_TM, d), F32),
                        pltpu.SemaphoreType.DMA(()), pltpu.SemaphoreType.DMA(())],
        compiler_params=_params(("arbitrary",)),
        name="ffn_rest",
    )(x, gain, w13b, w2b, final_gain, y0)


def _gmlp_kernel(x_ref, g_ref, win_hbm, lng_ref, lnb_ref, ws_ref, bs_ref, wout_hbm, o_ref,
                 win_ref, wout_ref, u_ref, sem, v_ref, uf_ref, *, layer):
    e = GMLP_WIDTH
    half = GMLP_TM // 2

    @pl.when(pl.program_id(0) == 0)
    def _():
        _load_weight(win_hbm.at[layer], win_ref, u_ref, sem, col0=0)
        _load_weight(win_hbm.at[layer], win_ref, u_ref, sem, col0=e)
        _load_weight(wout_hbm.at[layer], wout_ref, u_ref, sem)

    x = x_ref[...]
    h = _rms(x, g_ref[...]).astype(BF16)

    def gelu(z):
        return 0.5 * z * (1.0 + lax.erf(z * (2.0 ** -0.5)))

    v = gelu(_dot(h, win_ref[:, e:]))
    u = gelu(_dot(h, win_ref[:, :e]))
    u_ref[0] = u[:half]
    u_ref[1] = u[half:]
    mu = jnp.mean(v, axis=-1, keepdims=True)
    vc = v - mu
    var = jnp.mean(vc * vc, axis=-1, keepdims=True)
    v_ref[...] = (vc * lax.rsqrt(var + LN_EPS) * lng_ref[...] + lnb_ref[...]).astype(BF16)

    t_idx = lax.broadcasted_iota(jnp.int32, (CHUNK, CHUNK), 0)
    s_idx = lax.broadcasted_iota(jnp.int32, (CHUNK, CHUNK), 1)
    causal = t_idx >= s_idx
    for g in range(GMLP_GROUPS):
        w = jnp.where(causal, ws_ref[g], 0.0).astype(BF16)
        cols = slice(g * GMLP_GROUP_DIM, (g + 1) * GMLP_GROUP_DIM)
        for c in range(GMLP_TM // CHUNK):
            rows = slice(c * CHUNK, (c + 1) * CHUNK)
            r0 = (c * CHUNK) % half
            f = _dot(w, v_ref[rows, cols]) + bs_ref[g]
            uf_ref[rows, cols] = (u_ref[(c * CHUNK) // half, r0:r0 + CHUNK, cols] * f).astype(BF16)
    o_ref[...] = x + _dot(uf_ref[...], wout_ref[...])


def _gmlp(x, gain, w_in, ln_g, ln_b, w_s, b_s, w_out, layer):
    m, d = x.shape
    e = GMLP_WIDTH
    row = lambda i: (i, 0)
    bias = jnp.broadcast_to(b_s[:, :, None], (GMLP_GROUPS, CHUNK, GMLP_GROUP_DIM))
    assert d == e, "one staging width serves both weights"
    return pl.pallas_call(
        functools.partial(_gmlp_kernel, layer=layer),
        out_shape=jax.ShapeDtypeStruct((m, d), F32),
        grid=(m // GMLP_TM,),
        in_specs=[
            pl.BlockSpec((GMLP_TM, d), row),
            _resident((1, d)),
            _HBM,
            _resident((1, e)),
            _resident((1, e)),
            _resident((GMLP_GROUPS, CHUNK, CHUNK)),
            _resident((GMLP_GROUPS, CHUNK, GMLP_GROUP_DIM)),
            _HBM,
        ],
        out_specs=pl.BlockSpec((GMLP_TM, d), row),
        scratch_shapes=[pltpu.VMEM((d, 2 * e), BF16),
                        pltpu.VMEM((e, d), BF16),
                        pltpu.VMEM((2, GMLP_TM // 2, e), F32),
                        pltpu.SemaphoreType.DMA((2,)),
                        pltpu.VMEM((GMLP_TM, e), BF16),
                        pltpu.VMEM((GMLP_TM, e), BF16)],
        compiler_params=_params(("arbitrary",)),
        name="gmlp",
    )(x, gain, w_in, ln_g, ln_b, w_s, bias, w_out)


def _conv_kernel(x_ref, g_ref, win_hbm, cw_ref, wout_hbm, o_ref,
                 win_ref, win_stage, win_sem, wout_ref, wout_stage, wout_sem, z_ref, gb_ref, *, layer):
    d = D_MODEL
    tm = MIX_TM
    i = pl.program_id(0)

    @pl.when(i == 0)
    def _():
        _load_weight(win_hbm.at[layer], win_ref, win_stage, win_sem)
        _load_weight(wout_hbm.at[layer], wout_ref, wout_stage, wout_sem)
        z_ref[0:SUBLANES, :] = jnp.zeros((SUBLANES, d), F32)

    @pl.when(i > 0)
    def _():
        z_ref[0:SUBLANES, :] = z_ref[tm:tm + SUBLANES, :]

    x = x_ref[...]
    h = _rms(x, g_ref[...]).astype(BF16)
    for jb in range(d // CONV_TN):
        lo = jb * CONV_TN
        cols = slice(lo, lo + CONV_TN)
        gate_b = _dot(h, win_ref[:, lo:lo + CONV_TN])
        gate_c = _dot(h, win_ref[:, d + lo:d + lo + CONV_TN])
        val = _dot(h, win_ref[:, 2 * d + lo:2 * d + lo + CONV_TN])
        z = gate_c * val
        z_ref[SUBLANES:SUBLANES + tm, cols] = z
        conv = (cw_ref[0:1, cols] * z_ref[SUBLANES - 2:SUBLANES - 2 + tm, cols]
                + cw_ref[1:2, cols] * z_ref[SUBLANES - 1:SUBLANES - 1 + tm, cols]
                + cw_ref[2:3, cols] * z)
        gb_ref[:, cols] = (gate_b * conv).astype(BF16)
    o_ref[...] = x + _dot(gb_ref[...], wout_ref[...])


def _short_conv(x, gain, w_in, conv_w, w_out, layer):
    m, d = x.shape
    row = lambda i: (i, 0)
    return pl.pallas_call(
        functools.partial(_conv_kernel, layer=layer),
        out_shape=jax.ShapeDtypeStruct((m, d), F32),
        grid=(m // MIX_TM,),
        in_specs=[
            pl.BlockSpec((MIX_TM, d), row),
            _resident((1, d)),
            _HBM,
            _resident((CONV_WIDTH, d)),
            _HBM,
        ],
        out_specs=pl.BlockSpec((MIX_TM, d), row),
        scratch_shapes=[*_stage_scratch((d, 3 * d)),
                        *_stage_scratch((d, d)),
                        pltpu.VMEM((MIX_TM + SUBLANES, d), F32),
                        pltpu.VMEM((MIX_TM, d), BF16)],
        compiler_params=_params(("arbitrary",)),
        name="short_conv",
    )(x, gain, w_in, conv_w, w_out)


def _kv_kernel(mem_ref, g_ref, wkv_ref, kv_ref):
    mem_n = _rms(mem_ref[...], g_ref[...]).astype(BF16)
    kv_ref[...] = _dot(mem_n, wkv_ref[...].astype(BF16)).astype(BF16)


def _kv_proj(mem, gain, wkv, layer):
    mlen, d = mem.shape
    n = wkv.shape[2]
    return pl.pallas_call(
        _kv_kernel,
        out_shape=jax.ShapeDtypeStruct((mlen, n), BF16),
        grid=(n // KV_TN,),
        in_specs=[
            pl.BlockSpec((mlen, d), lambda j: (0, 0)),
            pl.BlockSpec((1, d), lambda j: (0, 0)),
            pl.BlockSpec((None, d, KV_TN), lambda j: (layer, 0, j)),
        ],
        out_specs=pl.BlockSpec((mlen, KV_TN), lambda j: (0, j)),
        compiler_params=_params(("parallel",)),
        name="kv_proj",
    )(mem, gain, wkv)


def _xattn_kernel(x_ref, xn_ref, g_ref, wq_hbm, k_ref, v_ref, wo_hbm, o_ref,
                  wq_ref, wo_ref, stage, sem, qa_ref, qb_ref, oa_ref, ob_ref, *, layer):
    hd = XATTN_HEAD_DIM
    tm = ATT_TM

    def normed(x):
        return _rms(x, g_ref[...]).astype(BF16)

    def attend(q_ref, x, ob_scr, h_next, q_next):
        for hh in range(XATTN_HEADS):
            cols = slice(hh * hd, (hh + 1) * hd)
            s = lax.dot_general(q_ref[:, cols], k_ref[:, cols], (((1,), (1,)), ((), ())),
                                preferred_element_type=F32) * (hd ** -0.5)
            q_next[:, cols] = _dot(h_next, wq_ref[:, cols]).astype(BF16)
            p = jnp.exp(s - jnp.max(s, axis=-1, keepdims=True))
            p = p / jnp.sum(p, axis=-1, keepdims=True)
            ob_scr[:, cols] = _dot(p.astype(BF16), v_ref[:, cols]).astype(BF16)
        return x + _dot(ob_scr[...], wo_ref[...])

    @pl.when(pl.program_id(0) == 0)
    def _():
        _load_weight(wq_hbm.at[layer], wq_ref, stage, sem)
        _load_weight(wo_hbm.at[layer], wo_ref, stage, sem)
        qa_ref[...] = _dot(normed(x_ref[0:tm, :]), wq_ref[...]).astype(BF16)

    x0 = x_ref[0:tm, :]
    x1 = x_ref[tm:2 * tm, :]
    o_ref[0:tm, :] = attend(qa_ref, x0, oa_ref, normed(x1), qb_ref)
    o_ref[tm:2 * tm, :] = attend(qb_ref, x1, ob_ref, normed(xn_ref[...]), qa_ref)


def _xattn(x, gain, wq, kv, wo, layer):
    m, d = x.shape
    mlen = kv.shape[0]
    n = m // (2 * ATT_TM)
    row = lambda i: (i, 0)
    w_scratch, stage, sem = _stage_scratch((d, d))
    half = pltpu.VMEM((ATT_TM, d), BF16)
    return pl.pallas_call(
        functools.partial(_xattn_kernel, layer=layer),
        out_shape=jax.ShapeDtypeStruct((m, d), F32),
        grid=(n,),
        in_specs=[
            pl.BlockSpec((2 * ATT_TM, d), row),
            pl.BlockSpec((ATT_TM, d), lambda i: (jnp.minimum(2 * i + 2, 2 * n - 1), 0)),
            _resident((1, d)),
            _HBM,
            pl.BlockSpec((mlen, d), lambda i: (0, 0), pipeline_mode=pl.Buffered(1)),
            pl.BlockSpec((mlen, d), lambda i: (0, 1), pipeline_mode=pl.Buffered(1)),
            _HBM,
        ],
        out_specs=pl.BlockSpec((2 * ATT_TM, d), row),
        scratch_shapes=[w_scratch, w_scratch, stage, sem, half, half, half, half],
        compiler_params=_params(("arbitrary",)),
        name="xattn",
    )(x, x, gain, wq, kv, kv, wo)


def kernel(x, mem, ffn1_norm, ffn1_w13, ffn1_w2, mix_norm, gmlp_w_in, gmlp_ln_g, gmlp_ln_b, gmlp_w_s, gmlp_b_s, gmlp_w_out, conv_w_in, conv_w, conv_w_out, xattn_norm, mem_norm, xattn_wq, xattn_wkv, xattn_wo, ffn2_norm, ffn2_w13, ffn2_w2, final_norm):
    bsz, seq, d = x.shape
    assert (bsz, seq, d) == (1, SEQ, D_MODEL), "conv carry assumes one sequence laid out along rows"
    xs = x.reshape(seq, d)
    mem2 = mem.reshape(MEM_LEN, d)
    vec = lambda v: v.reshape(1, -1)

    for i in range(DEPTH):
        xs = _ffn(xs, vec(ffn1_norm[i]), ffn1_w13, ffn1_w2, i)
        j = i // 2
        if i % 2 == 0:
            xs = _gmlp(xs, vec(mix_norm[i]), gmlp_w_in, vec(gmlp_ln_g[j]), vec(gmlp_ln_b[j]),
                       gmlp_w_s[j], gmlp_b_s[j], gmlp_w_out, j)
        else:
            xs = _short_conv(xs, vec(mix_norm[i]), conv_w_in, conv_w[j], conv_w_out, j)
        kv = _kv_proj(mem2, vec(mem_norm[i]), xattn_wkv, i)
        xs = _xattn(xs, vec(xattn_norm[i]), xattn_wq, kv, xattn_wo, i)
        last = i == DEPTH - 1
        xs = _ffn(xs, vec(ffn2_norm[i]), ffn2_w13, ffn2_w2, i,
                  final_gain=vec(final_norm) if last else None)
    return xs.reshape(bsz, seq, d)
```

```python
import functools

import jax
import jax.numpy as jnp
from jax import lax
from jax.experimental import pallas as pl
from jax.experimental.pallas import tpu as pltpu

D_MODEL = 2048
SEQ = 8192
DEPTH = 2
MEM_LEN = 256
D_FF = 5632
CHUNK = 128
GMLP_WIDTH = 2048
GMLP_GROUPS = 8
GMLP_GROUP_DIM = GMLP_WIDTH // GMLP_GROUPS
CONV_WIDTH = 3
XATTN_HEADS = 4
XATTN_HEAD_DIM = D_MODEL // XATTN_HEADS
RMS_EPS = 1e-6
LN_EPS = 1e-5

F32 = jnp.float32
BF16 = jnp.bfloat16

V7X_VMEM_LIMIT_BYTES = 60 * 1024 * 1024
SUBLANES = 8

FFN_TM = 1024
FFN_TF = 512
FFN_FIRST_ROWS = FFN_TM
NORM_ROWS = 256
MIX_TM = 256
GMLP_TM = 512
ATT_TM = 256
CONV_TN = 512
KV_TN = 1024
STAGE_BYTES = 2 * 1024 * 1024
CONV_STAGE_BYTES = 512 * 1024


def _rms(x, g):
    return x * lax.rsqrt(jnp.mean(x * x, axis=-1, keepdims=True) + RMS_EPS) * g


def _dot(a, b):
    return jnp.dot(a, b, preferred_element_type=F32)


def _params(semantics):
    return pltpu.CompilerParams(dimension_semantics=semantics,
                                vmem_limit_bytes=V7X_VMEM_LIMIT_BYTES)


def _resident(shape):
    return pl.BlockSpec(shape, lambda *_: (0,) * len(shape), pipeline_mode=pl.Buffered(1))


_HBM = pl.BlockSpec(memory_space=pl.ANY)


def _stage_rows(w_shape, slot_bytes):
    rows = slot_bytes // (w_shape[1] * 4) // 16 * 16
    while w_shape[0] % rows:
        rows -= 16
    return rows


def _stage_scratch(w_shape, slot_bytes=STAGE_BYTES):
    return [pltpu.VMEM(w_shape, BF16),
            pltpu.VMEM((2, _stage_rows(w_shape, slot_bytes), w_shape[1]), F32),
            pltpu.SemaphoreType.DMA((2,))]


def _load_weight(w_hbm, w_vmem, stage, sem, col0=0):
    rows, width = stage.shape[1], stage.shape[2]
    n = w_hbm.shape[0] // rows
    cols = slice(col0, col0 + width)

    def copy(c, slot):
        return pltpu.make_async_copy(w_hbm.at[pl.ds(c * rows, rows), cols], stage.at[slot], sem.at[slot])

    copy(0, 0).start()

    def body(c, carry):
        slot = lax.rem(c, 2)

        @pl.when(c + 1 < n)
        def _():
            copy(c + 1, 1 - slot).start()

        copy(c, slot).wait()
        w_vmem[pl.ds(pl.multiple_of(c * rows, rows), rows), cols] = stage[slot].astype(BF16)
        return carry

    lax.fori_loop(0, n, body, 0)


def _swiglu(h, w13, w2):
    tf = w13.shape[1] // 2
    gu = _dot(h, w13)
    gate = gu[:, :tf]
    up = gu[:, tf:]
    act = (gate * (0.5 / (1.0 + jnp.exp(-gate))) * up).astype(BF16)
    return _dot(act, w2)


def _ffn_first_kernel(x_hbm, g_ref, w1_ref, w3_ref, w2_ref, fg_ref, o_ref, w13b_ref, w2b_ref,
                      h_ref, sem, *, final_norm):
    j = pl.program_id(0)

    @pl.when(j == 0)
    def _():
        copy = pltpu.make_async_copy(x_hbm.at[pl.ds(0, FFN_FIRST_ROWS)], o_ref, sem)
        copy.start()
        copy.wait()
        for r in range(0, FFN_FIRST_ROWS, NORM_ROWS):
            h_ref[r:r + NORM_ROWS, :] = _rms(o_ref[r:r + NORM_ROWS, :], g_ref[...]).astype(BF16)

    w13 = jnp.concatenate([w1_ref[...].astype(BF16), w3_ref[...].astype(BF16)], axis=1)
    w2 = w2_ref[...].astype(BF16)
    w13b_ref[...] = w13
    w2b_ref[...] = w2
    o_ref[...] += _swiglu(h_ref[...], w13, w2)

    if final_norm:
        @pl.when(j == pl.num_programs(0) - 1)
        def _():
            for r in range(0, FFN_FIRST_ROWS, NORM_ROWS):
                o_ref[r:r + NORM_ROWS, :] = _rms(o_ref[r:r + NORM_ROWS, :], fg_ref[...])


def _ffn_rest_kernel(x_hbm, g_ref, w13_ref, w2_ref, fg_ref, y0_hbm, o_ref, h_ref, xs_ref, sem, xsem,
                     *, nf, n_first, n_blocks, final_norm):
    tf = FFN_TF
    n_pair = nf // 2
    steps = n_pair + nf % 2
    s = pl.program_id(0)
    t = jnp.maximum(s - n_first, 0)
    j = lax.rem(t, steps)
    blk = n_first + t // steps

    def x_copy(b):
        return pltpu.make_async_copy(x_hbm.at[pl.ds(pl.multiple_of(b * FFN_TM, FFN_TM), FFN_TM)], xs_ref, xsem)

    @pl.when(s < n_first)
    def _():
        @pl.when(s == 0)
        def _():
            x_copy(n_first).start()

        copy = pltpu.make_async_copy(y0_hbm.at[pl.ds(pl.multiple_of(s * FFN_TM, FFN_TM), FFN_TM)], o_ref, sem)
        copy.start()
        copy.wait()

    @pl.when(s >= n_first)
    def _():
        @pl.when(j == 0)
        def _():
            x_copy(blk).wait()
            h_ref[...] = _rms(xs_ref[...], g_ref[...]).astype(BF16)
            o_ref[...] = xs_ref[...] + _swiglu(h_ref[...], w13_ref[0], w2_ref[0:tf, :])
            o_ref[...] += _swiglu(h_ref[...], w13_ref[1], w2_ref[tf:2 * tf, :])

            @pl.when(blk + 1 < n_blocks)
            def _():
                x_copy(blk + 1).start()

        @pl.when(jnp.logical_and(j > 0, j < n_pair))
        def _():
            o_ref[...] += _swiglu(h_ref[...], w13_ref[0], w2_ref[0:tf, :])
            o_ref[...] += _swiglu(h_ref[...], w13_ref[1], w2_ref[tf:2 * tf, :])

        if nf % 2:
            @pl.when(j == n_pair)
            def _():
                o_ref[...] += _swiglu(h_ref[...], w13_ref[0], w2_ref[0:tf, :])

        if final_norm:
            @pl.when(j == steps - 1)
            def _():
                o_ref[...] = _rms(o_ref[...], fg_ref[...])


def _ffn(x, gain, w13, w2, layer, final_gain=None):
    m, d = x.shape
    f = w2.shape[1]
    final_norm = final_gain is not None
    if final_gain is None:
        final_gain = gain
    vec = pl.BlockSpec((1, d), lambda s: (0, 0))

    nf = f // FFN_TF
    y0, w13b, w2b = pl.pallas_call(
        functools.partial(_ffn_first_kernel, final_norm=final_norm),
        out_shape=[jax.ShapeDtypeStruct((FFN_FIRST_ROWS, d), F32),
                   jax.ShapeDtypeStruct((nf, d, 2 * FFN_TF), BF16),
                   jax.ShapeDtypeStruct((f, d), BF16)],
        grid=(nf,),
        in_specs=[
            _HBM,
            vec,
            pl.BlockSpec((None, d, FFN_TF), lambda j: (layer, 0, j)),
            pl.BlockSpec((None, d, FFN_TF), lambda j: (layer, 0, j + nf)),
            pl.BlockSpec((None, FFN_TF, d), lambda j: (layer, j, 0)),
            vec,
        ],
        out_specs=[
            pl.BlockSpec((FFN_FIRST_ROWS, d), lambda j: (0, 0)),
            pl.BlockSpec((None, d, 2 * FFN_TF), lambda j: (j, 0, 0)),
            pl.BlockSpec((FFN_TF, d), lambda j: (j, 0)),
        ],
        scratch_shapes=[pltpu.VMEM((FFN_FIRST_ROWS, d), BF16), pltpu.SemaphoreType.DMA(())],
        compiler_params=_params(("arbitrary",)),
        name="ffn_first",
    )(x, gain, w13, w13, w2, final_gain)

    n_first = FFN_FIRST_ROWS // FFN_TM
    n_blocks = m // FFN_TM
    steps = nf // 2 + nf % 2
    assert n_first == 1
    rest = lambda s: jnp.maximum(s - n_first, 0)
    out_row = lambda s: (jnp.where(s < n_first, s, n_first + rest(s) // steps), 0)
    w_pair = lambda s: lax.rem(rest(s), steps)
    return pl.pallas_call(
        functools.partial(_ffn_rest_kernel, nf=nf, n_first=n_first, n_blocks=n_blocks, final_norm=final_norm),
        out_shape=jax.ShapeDtypeStruct((m, d), F32),
        grid=(n_first + (n_blocks - n_first) * steps,),
        in_specs=[
            _HBM,
            vec,
            pl.BlockSpec((2, d, 2 * FFN_TF), lambda s: (w_pair(s), 0, 0)),
            pl.BlockSpec((2 * FFN_TF, d), lambda s: (w_pair(s), 0)),
            vec,
            _HBM,
        ],
        out_specs=pl.BlockSpec((FFN_TM, d), out_row),
        scratch_shapes=[pltpu.VMEM((FFN_TM, d), BF16), pltpu.VMEM((FFN_TM, d), F32),
                        pltpu.SemaphoreType.DMA(()), pltpu.SemaphoreType.DMA(())],
        compiler_params=_params(("arbitrary",)),
        name="ffn_rest",
    )(x, gain, w13b, w2b, final_gain, y0)


def _gmlp_kernel(x_ref, g_ref, win_hbm, lng_ref, lnb_ref, ws_ref, bs_ref, wout_hbm, o_ref,
                 win_ref, wout_ref, u_ref, sem, v_ref, uf_ref, *, layer):
    e = GMLP_WIDTH
    half = GMLP_TM // 2

    @pl.when(pl.program_id(0) == 0)
    def _():
        _load_weight(win_hbm.at[layer], win_ref, u_ref, sem, col0=0)
        _load_weight(win_hbm.at[layer], win_ref, u_ref, sem, col0=e)
        _load_weight(wout_hbm.at[layer], wout_ref, u_ref, sem)

    x = x_ref[...]
    h = _rms(x, g_ref[...]).astype(BF16)

    def gelu(z):
        return 0.5 * z * (1.0 + lax.erf(z * (2.0 ** -0.5)))

    v = gelu(_dot(h, win_ref[:, e:]))
    u = gelu(_dot(h, win_ref[:, :e]))
    u_ref[0] = u[:half]
    u_ref[1] = u[half:]
    mu = jnp.mean(v, axis=-1, keepdims=True)
    vc = v - mu
    var = jnp.mean(vc * vc, axis=-1, keepdims=True)
    v_ref[...] = (vc * lax.rsqrt(var + LN_EPS) * lng_ref[...] + lnb_ref[...]).astype(BF16)

    t_idx = lax.broadcasted_iota(jnp.int32, (CHUNK, CHUNK), 0)
    s_idx = lax.broadcasted_iota(jnp.int32, (CHUNK, CHUNK), 1)
    causal = t_idx >= s_idx
    for g in range(GMLP_GROUPS):
        w = jnp.where(causal, ws_ref[g], 0.0).astype(BF16)
        cols = slice(g * GMLP_GROUP_DIM, (g + 1) * GMLP_GROUP_DIM)
        for c in range(GMLP_TM // CHUNK):
            rows = slice(c * CHUNK, (c + 1) * CHUNK)
            r0 = (c * CHUNK) % half
            f = _dot(w, v_ref[rows, cols]) + bs_ref[g]
            uf_ref[rows, cols] = (u_ref[(c * CHUNK) // half, r0:r0 + CHUNK, cols] * f).astype(BF16)
    o_ref[...] = x + _dot(uf_ref[...], wout_ref[...])


def _gmlp(x, gain, w_in, ln_g, ln_b, w_s, b_s, w_out, layer):
    m, d = x.shape
    e = GMLP_WIDTH
    row = lambda i: (i, 0)
    bias = jnp.broadcast_to(b_s[:, :, None], (GMLP_GROUPS, CHUNK, GMLP_GROUP_DIM))
    assert d == e, "one staging width serves both weights"
    return pl.pallas_call(
        functools.partial(_gmlp_kernel, layer=layer),
        out_shape=jax.ShapeDtypeStruct((m, d), F32),
        grid=(m // GMLP_TM,),
        in_specs=[
            pl.BlockSpec((GMLP_TM, d), row),
            _resident((1, d)),
            _HBM,
            _resident((1, e)),
            _resident((1, e)),
            _resident((GMLP_GROUPS, CHUNK, CHUNK)),
            _resident((GMLP_GROUPS, CHUNK, GMLP_GROUP_DIM)),
            _HBM,
        ],
        out_specs=pl.BlockSpec((GMLP_TM, d), row),
        scratch_shapes=[pltpu.VMEM((d, 2 * e), BF16),
                        pltpu.VMEM((e, d), BF16),
                        pltpu.VMEM((2, GMLP_TM // 2, e), F32),
                        pltpu.SemaphoreType.DMA((2,)),
                        pltpu.VMEM((GMLP_TM, e), BF16),
                        pltpu.VMEM((GMLP_TM, e), BF16)],
        compiler_params=_params(("arbitrary",)),
        name="gmlp",
    )(x, gain, w_in, ln_g, ln_b, w_s, bias, w_out)


def _conv_kernel(x_ref, g_ref, win_hbm, cw_ref, wout_hbm, o_ref,
                 win_ref, win_stage, win_sem, wout_ref, wout_stage, wout_sem, z_ref, gb_ref, *, layer):
    d = D_MODEL
    tm = MIX_TM
    i = pl.program_id(0)

    @pl.when(i == 0)
    def _():
        _load_weight(win_hbm.at[layer], win_ref, win_stage, win_sem)
        _load_weight(wout_hbm.at[layer], wout_ref, wout_stage, wout_sem)
        z_ref[0:SUBLANES, :] = jnp.zeros((SUBLANES, d), F32)

    @pl.when(i > 0)
    def _():
        z_ref[0:SUBLANES, :] = z_ref[tm:tm + SUBLANES, :]

    for k in range(2):
        rows = slice(k * tm, (k + 1) * tm)
        if k:
            z_ref[0:SUBLANES, :] = z_ref[tm:tm + SUBLANES, :]
        x = x_ref[rows, :]
        h = _rms(x, g_ref[...]).astype(BF16)
        for jb in range(d // CONV_TN):
            lo = jb * CONV_TN
            cols = slice(lo, lo + CONV_TN)
            gate_b = _dot(h, win_ref[:, lo:lo + CONV_TN])
            gate_c = _dot(h, win_ref[:, d + lo:d + lo + CONV_TN])
            val = _dot(h, win_ref[:, 2 * d + lo:2 * d + lo + CONV_TN])
            z = gate_c * val
            z_ref[SUBLANES:SUBLANES + tm, cols] = z
            conv = (cw_ref[0:1, cols] * z_ref[SUBLANES - 2:SUBLANES - 2 + tm, cols]
                    + cw_ref[1:2, cols] * z_ref[SUBLANES - 1:SUBLANES - 1 + tm, cols]
                    + cw_ref[2:3, cols] * z)
            gb_ref[k, :, cols] = (gate_b * conv).astype(BF16)
        o_ref[rows, :] = x + _dot(gb_ref[k], wout_ref[...])


def _short_conv(x, gain, w_in, conv_w, w_out, layer):
    m, d = x.shape
    row = lambda i: (i, 0)
    return pl.pallas_call(
        functools.partial(_conv_kernel, layer=layer),
        out_shape=jax.ShapeDtypeStruct((m, d), F32),
        grid=(m // (2 * MIX_TM),),
        in_specs=[
            pl.BlockSpec((2 * MIX_TM, d), row),
            _resident((1, d)),
            _HBM,
            _resident((CONV_WIDTH, d)),
            _HBM,
        ],
        out_specs=pl.BlockSpec((2 * MIX_TM, d), row),
        scratch_shapes=[*_stage_scratch((d, 3 * d), CONV_STAGE_BYTES),
                        *_stage_scratch((d, d), CONV_STAGE_BYTES),
                        pltpu.VMEM((MIX_TM + SUBLANES, d), F32),
                        pltpu.VMEM((2, MIX_TM, d), BF16)],
        compiler_params=_params(("arbitrary",)),
        name="short_conv",
    )(x, gain, w_in, conv_w, w_out)


def _kv_kernel(mem_ref, g_ref, wkv_ref, kv_ref):
    mem_n = _rms(mem_ref[...], g_ref[...]).astype(BF16)
    kv_ref[...] = _dot(mem_n, wkv_ref[...].astype(BF16)).astype(BF16)


def _kv_proj(mem, gain, wkv, layer):
    mlen, d = mem.shape
    n = wkv.shape[2]
    return pl.pallas_call(
        _kv_kernel,
        out_shape=jax.ShapeDtypeStruct((mlen, n), BF16),
        grid=(n // KV_TN,),
        in_specs=[
            pl.BlockSpec((mlen, d), lambda j: (0, 0)),
            pl.BlockSpec((1, d), lambda j: (0, 0)),
            pl.BlockSpec((None, d, KV_TN), lambda j: (layer, 0, j)),
        ],
        out_specs=pl.BlockSpec((mlen, KV_TN), lambda j: (0, j)),
        compiler_params=_params(("parallel",)),
        name="kv_proj",
    )(mem, gain, wkv)


def _xattn_kernel(x_ref, xn_ref, g_ref, wq_hbm, k_ref, v_ref, wo_hbm, o_ref,
                  wq_ref, wo_ref, stage, sem, qa_ref, qb_ref, oa_ref, ob_ref, *, layer):
    hd = XATTN_HEAD_DIM
    tm = ATT_TM

    def normed(x):
        return _rms(x, g_ref[...]).astype(BF16)

    def attend(q_ref, x, ob_scr, h_next, q_next):
        for hh in range(XATTN_HEADS):
            cols = slice(hh * hd, (hh + 1) * hd)
            s = lax.dot_general(q_ref[:, cols], k_ref[:, cols], (((1,), (1,)), ((), ())),
                                preferred_element_type=F32) * (hd ** -0.5)
            q_next[:, cols] = _dot(h_next, wq_ref[:, cols]).astype(BF16)
            p = jnp.exp(s - jnp.max(s, axis=-1, keepdims=True))
            p = p / jnp.sum(p, axis=-1, keepdims=True)
            ob_scr[:, cols] = _dot(p.astype(BF16), v_ref[:, cols]).astype(BF16)
        return x + _dot(ob_scr[...], wo_ref[...])

    @pl.when(pl.program_id(0) == 0)
    def _():
        _load_weight(wq_hbm.at[layer], wq_ref, stage, sem)
        _load_weight(wo_hbm.at[layer], wo_ref, stage, sem)
        qa_ref[...] = _dot(normed(x_ref[0:tm, :]), wq_ref[...]).astype(BF16)

    x0 = x_ref[0:tm, :]
    x1 = x_ref[tm:2 * tm, :]
    o_ref[0:tm, :] = attend(qa_ref, x0, oa_ref, normed(x1), qb_ref)
    o_ref[tm:2 * tm, :] = attend(qb_ref, x1, ob_ref, normed(xn_ref[...]), qa_ref)


def _xattn(x, gain, wq, kv, wo, layer):
    m, d = x.shape
    mlen = kv.shape[0]
    n = m // (2 * ATT_TM)
    row = lambda i: (i, 0)
    w_scratch, stage, sem = _stage_scratch((d, d))
    half = pltpu.VMEM((ATT_TM, d), BF16)
    return pl.pallas_call(
        functools.partial(_xattn_kernel, layer=layer),
        out_shape=jax.ShapeDtypeStruct((m, d), F32),
        grid=(n,),
        in_specs=[
            pl.BlockSpec((2 * ATT_TM, d), row),
            pl.BlockSpec((ATT_TM, d), lambda i: (jnp.minimum(2 * i + 2, 2 * n - 1), 0)),
            _resident((1, d)),
            _HBM,
            pl.BlockSpec((mlen, d), lambda i: (0, 0), pipeline_mode=pl.Buffered(1)),
            pl.BlockSpec((mlen, d), lambda i: (0, 1), pipeline_mode=pl.Buffered(1)),
            _HBM,
        ],
        out_specs=pl.BlockSpec((2 * ATT_TM, d), row),
        scratch_shapes=[w_scratch, w_scratch, stage, sem, half, half, half, half],
        compiler_params=_params(("arbitrary",)),
        name="xattn",
    )(x, x, gain, wq, kv, kv, wo)


def kernel(x, mem, ffn1_norm, ffn1_w13, ffn1_w2, mix_norm, gmlp_w_in, gmlp_ln_g, gmlp_ln_b, gmlp_w_s, gmlp_b_s, gmlp_w_out, conv_w_in, conv_w, conv_w_out, xattn_norm, mem_norm, xattn_wq, xattn_wkv, xattn_wo, ffn2_norm, ffn2_w13, ffn2_w2, final_norm):
    bsz, seq, d = x.shape
    assert (bsz, seq, d) == (1, SEQ, D_MODEL), "conv carry assumes one sequence laid out along rows"
    xs = x.reshape(seq, d)
    mem2 = mem.reshape(MEM_LEN, d)
    vec = lambda v: v.reshape(1, -1)

    for i in range(DEPTH):
        xs = _ffn(xs, vec(ffn1_norm[i]), ffn1_w13, ffn1_w2, i)
        j = i // 2
        if i % 2 == 0:
            xs = _gmlp(xs, vec(mix_norm[i]), gmlp_w_in, vec(gmlp_ln_g[j]), vec(gmlp_ln_b[j]),
                       gmlp_w_s[j], gmlp_b_s[j], gmlp_w_out, j)
        else:
            xs = _short_conv(xs, vec(mix_norm[i]), conv_w_in, conv_w[j], conv_w_out, j)
        kv = _kv_proj(mem2, vec(mem_norm[i]), xattn_wkv, i)
        xs = _xattn(xs, vec(xattn_norm[i]), xattn_wq, kv, xattn_wo, i)
        last = i == DEPTH - 1
        xs = _ffn(xs, vec(ffn2_norm[i]), ffn2_w13, ffn2_w2, i,
                  final_gain=vec(final_norm) if last else None)
    return xs.reshape(bsz, seq, d)
```

```python
import functools

import jax
import jax.numpy as jnp
from jax import lax
from jax.experimental import pallas as pl
from jax.experimental.pallas import tpu as pltpu

D_MODEL = 2048
SEQ = 8192
DEPTH = 2
MEM_LEN = 256
D_FF = 5632
CHUNK = 128
GMLP_WIDTH = 2048
GMLP_GROUPS = 8
GMLP_GROUP_DIM = GMLP_WIDTH // GMLP_GROUPS
CONV_WIDTH = 3
XATTN_HEADS = 4
XATTN_HEAD_DIM = D_MODEL // XATTN_HEADS
RMS_EPS = 1e-6
LN_EPS = 1e-5

F32 = jnp.float32
BF16 = jnp.bfloat16

V7X_VMEM_LIMIT_BYTES = 60 * 1024 * 1024
SUBLANES = 8

FFN_TM = 1024
FFN_TF = 512
FFN_FIRST_ROWS = FFN_TM
NORM_ROWS = 256
MIX_TM = 256
GMLP_TM = 512
ATT_TM = 256
CONV_TN = 512
KV_TN = 1024
STAGE_BYTES = 2 * 1024 * 1024


def _rms(x, g):
    return x * lax.rsqrt(jnp.mean(x * x, axis=-1, keepdims=True) + RMS_EPS) * g


def _dot(a, b):
    return jnp.dot(a, b, preferred_element_type=F32)


def _params(semantics):
    return pltpu.CompilerParams(dimension_semantics=semantics,
                                vmem_limit_bytes=V7X_VMEM_LIMIT_BYTES)


def _resident(shape):
    return pl.BlockSpec(shape, lambda *_: (0,) * len(shape), pipeline_mode=pl.Buffered(1))


_HBM = pl.BlockSpec(memory_space=pl.ANY)


def _stage_rows(w_shape):
    rows = STAGE_BYTES // (w_shape[1] * 4) // 16 * 16
    while w_shape[0] % rows:
        rows -= 16
    return rows


def _stage_scratch(w_shape):
    return [pltpu.VMEM(w_shape, BF16),
            pltpu.VMEM((2, _stage_rows(w_shape), w_shape[1]), F32),
            pltpu.SemaphoreType.DMA((2,))]


def _load_weight(w_hbm, w_vmem, stage, sem, col0=0):
    if len(stage.shape) == 3:
        rows, width = stage.shape[1], stage.shape[2]
        slot_ref = lambda slot: stage.at[slot]
    else:
        rows, width = stage.shape[0] // 2, stage.shape[1]
        slot_ref = lambda slot: stage.at[pl.ds(pl.multiple_of(slot * rows, rows), rows)]
    n = w_hbm.shape[0] // rows
    cols = slice(col0, col0 + width)

    def copy(c, slot):
        return pltpu.make_async_copy(w_hbm.at[pl.ds(c * rows, rows), cols], slot_ref(slot), sem.at[slot])

    copy(0, 0).start()

    def body(c, carry):
        slot = lax.rem(c, 2)

        @pl.when(c + 1 < n)
        def _():
            copy(c + 1, 1 - slot).start()

        copy(c, slot).wait()
        w_vmem[pl.ds(pl.multiple_of(c * rows, rows), rows), cols] = slot_ref(slot)[...].astype(BF16)
        return carry

    lax.fori_loop(0, n, body, 0)


def _swiglu(h, w13, w2):
    tf = w13.shape[1] // 2
    gu = _dot(h, w13)
    gate = gu[:, :tf]
    up = gu[:, tf:]
    act = (gate * (0.5 / (1.0 + jnp.exp(-gate))) * up).astype(BF16)
    return _dot(act, w2)


def _ffn_first_kernel(x_hbm, g_ref, w1_ref, w3_ref, w2_ref, fg_ref, o_ref, w13b_ref, w2b_ref,
                      h_ref, sem, *, final_norm):
    j = pl.program_id(0)

    @pl.when(j == 0)
    def _():
        copy = pltpu.make_async_copy(x_hbm.at[pl.ds(0, FFN_FIRST_ROWS)], o_ref, sem)
        copy.start()
        copy.wait()
        for r in range(0, FFN_FIRST_ROWS, NORM_ROWS):
            h_ref[r:r + NORM_ROWS, :] = _rms(o_ref[r:r + NORM_ROWS, :], g_ref[...]).astype(BF16)

    w13 = jnp.concatenate([w1_ref[...].astype(BF16), w3_ref[...].astype(BF16)], axis=1)
    w2 = w2_ref[...].astype(BF16)
    w13b_ref[...] = w13
    w2b_ref[...] = w2
    o_ref[...] += _swiglu(h_ref[...], w13, w2)

    if final_norm:
        @pl.when(j == pl.num_programs(0) - 1)
        def _():
            for r in range(0, FFN_FIRST_ROWS, NORM_ROWS):
                o_ref[r:r + NORM_ROWS, :] = _rms(o_ref[r:r + NORM_ROWS, :], fg_ref[...])


def _ffn_rest_kernel(x_hbm, g_ref, w13_ref, w2_ref, fg_ref, y0_hbm, o_ref, h_ref, xs_ref, sem, xsem,
                     *, nf, n_first, n_blocks, final_norm):
    tf = FFN_TF
    n_pair = nf // 2
    steps = n_pair + nf % 2
    s = pl.program_id(0)
    t = jnp.maximum(s - n_first, 0)
    j = lax.rem(t, steps)
    blk = n_first + t // steps

    def x_copy(b):
        return pltpu.make_async_copy(x_hbm.at[pl.ds(pl.multiple_of(b * FFN_TM, FFN_TM), FFN_TM)], xs_ref, xsem)

    @pl.when(s < n_first)
    def _():
        @pl.when(s == 0)
        def _():
            x_copy(n_first).start()

        copy = pltpu.make_async_copy(y0_hbm.at[pl.ds(pl.multiple_of(s * FFN_TM, FFN_TM), FFN_TM)], o_ref, sem)
        copy.start()
        copy.wait()

    @pl.when(s >= n_first)
    def _():
        @pl.when(j == 0)
        def _():
            x_copy(blk).wait()
            h_ref[...] = _rms(xs_ref[...], g_ref[...]).astype(BF16)
            o_ref[...] = xs_ref[...] + _swiglu(h_ref[...], w13_ref[0], w2_ref[0:tf, :])
            o_ref[...] += _swiglu(h_ref[...], w13_ref[1], w2_ref[tf:2 * tf, :])

            @pl.when(blk + 1 < n_blocks)
            def _():
                x_copy(blk + 1).start()

        @pl.when(jnp.logical_and(j > 0, j < n_pair))
        def _():
            o_ref[...] += _swiglu(h_ref[...], w13_ref[0], w2_ref[0:tf, :])
            o_ref[...] += _swiglu(h_ref[...], w13_ref[1], w2_ref[tf:2 * tf, :])

        if nf % 2:
            @pl.when(j == n_pair)
            def _():
                o_ref[...] += _swiglu(h_ref[...], w13_ref[0], w2_ref[0:tf, :])

        if final_norm:
            @pl.when(j == steps - 1)
            def _():
                o_ref[...] = _rms(o_ref[...], fg_ref[...])


def _ffn(x, gain, w13, w2, layer, final_gain=None):
    m, d = x.shape
    f = w2.shape[1]
    final_norm = final_gain is not None
    if final_gain is None:
        final_gain = gain
    vec = pl.BlockSpec((1, d), lambda s: (0, 0))

    nf = f // FFN_TF
    y0, w13b, w2b = pl.pallas_call(
        functools.partial(_ffn_first_kernel, final_norm=final_norm),
        out_shape=[jax.ShapeDtypeStruct((FFN_FIRST_ROWS, d), F32),
                   jax.ShapeDtypeStruct((nf, d, 2 * FFN_TF), BF16),
                   jax.ShapeDtypeStruct((f, d), BF16)],
        grid=(nf,),
        in_specs=[
            _HBM,
            vec,
            pl.BlockSpec((None, d, FFN_TF), lambda j: (layer, 0, j)),
            pl.BlockSpec((None, d, FFN_TF), lambda j: (layer, 0, j + nf)),
            pl.BlockSpec((None, FFN_TF, d), lambda j: (layer, j, 0)),
            vec,
        ],
        out_specs=[
            pl.BlockSpec((FFN_FIRST_ROWS, d), lambda j: (0, 0)),
            pl.BlockSpec((None, d, 2 * FFN_TF), lambda j: (j, 0, 0)),
            pl.BlockSpec((FFN_TF, d), lambda j: (j, 0)),
        ],
        scratch_shapes=[pltpu.VMEM((FFN_FIRST_ROWS, d), BF16), pltpu.SemaphoreType.DMA(())],
        compiler_params=_params(("arbitrary",)),
        name="ffn_first",
    )(x, gain, w13, w13, w2, final_gain)

    n_first = FFN_FIRST_ROWS // FFN_TM
    n_blocks = m // FFN_TM
    steps = nf // 2 + nf % 2
    assert n_first == 1
    rest = lambda s: jnp.maximum(s - n_first, 0)
    out_row = lambda s: (jnp.where(s < n_first, s, n_first + rest(s) // steps), 0)
    w_pair = lambda s: lax.rem(rest(s), steps)
    return pl.pallas_call(
        functools.partial(_ffn_rest_kernel, nf=nf, n_first=n_first, n_blocks=n_blocks, final_norm=final_norm),
        out_shape=jax.ShapeDtypeStruct((m, d), F32),
        grid=(n_first + (n_blocks - n_first) * steps,),
        in_specs=[
            _HBM,
            vec,
            pl.BlockSpec((2, d, 2 * FFN_TF), lambda s: (w_pair(s), 0, 0)),
            pl.BlockSpec((2 * FFN_TF, d), lambda s: (w_pair(s), 0)),
            vec,
            _HBM,
        ],
        out_specs=pl.BlockSpec((FFN_TM, d), out_row),
        scratch_shapes=[pltpu.VMEM((FFN_TM, d), BF16), pltpu.VMEM((FFN_TM, d), F32),
                        pltpu.SemaphoreType.DMA(()), pltpu.SemaphoreType.DMA(())],
        compiler_params=_params(("arbitrary",)),
        name="ffn_rest",
    )(x, gain, w13b, w2b, final_gain, y0)


def _gmlp_kernel(x_ref, g_ref, win_hbm, lng_ref, lnb_ref, ws_ref, bs_ref, wout_hbm, o_ref,
                 win_ref, wout_ref, u_ref, sem, v_ref, uf_ref, *, layer):
    e = GMLP_WIDTH
    half = GMLP_TM // 2

    @pl.when(pl.program_id(0) == 0)
    def _():
        _load_weight(win_hbm.at[layer], win_ref, u_ref, sem, col0=0)
        _load_weight(win_hbm.at[layer], win_ref, u_ref, sem, col0=e)
        _load_weight(wout_hbm.at[layer], wout_ref, u_ref, sem)

    x = x_ref[...]
    h = _rms(x, g_ref[...]).astype(BF16)

    def gelu(z):
        return 0.5 * z * (1.0 + lax.erf(z * (2.0 ** -0.5)))

    v = gelu(_dot(h, win_ref[:, e:]))
    u = gelu(_dot(h, win_ref[:, :e]))
    u_ref[0] = u[:half]
    u_ref[1] = u[half:]
    mu = jnp.mean(v, axis=-1, keepdims=True)
    vc = v - mu
    var = jnp.mean(vc * vc, axis=-1, keepdims=True)
    v_ref[...] = (vc * lax.rsqrt(var + LN_EPS) * lng_ref[...] + lnb_ref[...]).astype(BF16)

    t_idx = lax.broadcasted_iota(jnp.int32, (CHUNK, CHUNK), 0)
    s_idx = lax.broadcasted_iota(jnp.int32, (CHUNK, CHUNK), 1)
    causal = t_idx >= s_idx
    for g in range(GMLP_GROUPS):
        w = jnp.where(causal, ws_ref[g], 0.0).astype(BF16)
        cols = slice(g * GMLP_GROUP_DIM, (g + 1) * GMLP_GROUP_DIM)
        for c in range(GMLP_TM // CHUNK):
            rows = slice(c * CHUNK, (c + 1) * CHUNK)
            r0 = (c * CHUNK) % half
            f = _dot(w, v_ref[rows, cols]) + bs_ref[g]
            uf_ref[rows, cols] = (u_ref[(c * CHUNK) // half, r0:r0 + CHUNK, cols] * f).astype(BF16)
    o_ref[...] = x + _dot(uf_ref[...], wout_ref[...])


def _gmlp(x, gain, w_in, ln_g, ln_b, w_s, b_s, w_out, layer):
    m, d = x.shape
    e = GMLP_WIDTH
    row = lambda i: (i, 0)
    bias = jnp.broadcast_to(b_s[:, :, None], (GMLP_GROUPS, CHUNK, GMLP_GROUP_DIM))
    assert d == e, "one staging width serves both weights"
    return pl.pallas_call(
        functools.partial(_gmlp_kernel, layer=layer),
        out_shape=jax.ShapeDtypeStruct((m, d), F32),
        grid=(m // GMLP_TM,),
        in_specs=[
            pl.BlockSpec((GMLP_TM, d), row),
            _resident((1, d)),
            _HBM,
            _resident((1, e)),
            _resident((1, e)),
            _resident((GMLP_GROUPS, CHUNK, CHUNK)),
            _resident((GMLP_GROUPS, CHUNK, GMLP_GROUP_DIM)),
            _HBM,
        ],
        out_specs=pl.BlockSpec((GMLP_TM, d), row),
        scratch_shapes=[pltpu.VMEM((d, 2 * e), BF16),
                        pltpu.VMEM((e, d), BF16),
                        pltpu.VMEM((2, GMLP_TM // 2, e), F32),
                        pltpu.SemaphoreType.DMA((2,)),
                        pltpu.VMEM((GMLP_TM, e), BF16),
                        pltpu.VMEM((GMLP_TM, e), BF16)],
        compiler_params=_params(("arbitrary",)),
        name="gmlp",
    )(x, gain, w_in, ln_g, ln_b, w_s, bias, w_out)


def _conv_kernel(x_ref, g_ref, win_hbm, cw_ref, wout_hbm, o_ref,
                 win_ref, wout_ref, sem, z_ref, gb_ref, *, layer):
    d = D_MODEL
    tm = MIX_TM
    i = pl.program_id(0)

    @pl.when(i == 0)
    def _():
        for col0 in range(0, 3 * d, d):
            _load_weight(win_hbm.at[layer], win_ref, o_ref, sem, col0=col0)
        _load_weight(wout_hbm.at[layer], wout_ref, o_ref, sem)
        z_ref[0:SUBLANES, :] = jnp.zeros((SUBLANES, d), F32)

    @pl.when(i > 0)
    def _():
        z_ref[0:SUBLANES, :] = z_ref[tm:tm + SUBLANES, :]

    for k in range(2):
        rows = slice(k * tm, (k + 1) * tm)
        if k:
            z_ref[0:SUBLANES, :] = z_ref[tm:tm + SUBLANES, :]
        x = x_ref[rows, :]
        h = _rms(x, g_ref[...]).astype(BF16)
        for jb in range(d // CONV_TN):
            lo = jb * CONV_TN
            cols = slice(lo, lo + CONV_TN)
            gate_b = _dot(h, win_ref[:, lo:lo + CONV_TN])
            gate_c = _dot(h, win_ref[:, d + lo:d + lo + CONV_TN])
            val = _dot(h, win_ref[:, 2 * d + lo:2 * d + lo + CONV_TN])
            z = gate_c * val
            z_ref[SUBLANES:SUBLANES + tm, cols] = z
            conv = (cw_ref[0:1, cols] * z_ref[SUBLANES - 2:SUBLANES - 2 + tm, cols]
                    + cw_ref[1:2, cols] * z_ref[SUBLANES - 1:SUBLANES - 1 + tm, cols]
                    + cw_ref[2:3, cols] * z)
            gb_ref[k, :, cols] = (gate_b * conv).astype(BF16)
        o_ref[rows, :] = x + _dot(gb_ref[k], wout_ref[...])


def _short_conv(x, gain, w_in, conv_w, w_out, layer):
    m, d = x.shape
    row = lambda i: (i, 0)
    return pl.pallas_call(
        functools.partial(_conv_kernel, layer=layer),
        out_shape=jax.ShapeDtypeStruct((m, d), F32),
        grid=(m // (2 * MIX_TM),),
        in_specs=[
            pl.BlockSpec((2 * MIX_TM, d), row),
            _resident((1, d)),
            _HBM,
            _resident((CONV_WIDTH, d)),
            _HBM,
        ],
        out_specs=pl.BlockSpec((2 * MIX_TM, d), row),
        scratch_shapes=[pltpu.VMEM((d, 3 * d), BF16),
                        pltpu.VMEM((d, d), BF16),
                        pltpu.SemaphoreType.DMA((2,)),
                        pltpu.VMEM((MIX_TM + SUBLANES, d), F32),
                        pltpu.VMEM((2, MIX_TM, d), BF16)],
        compiler_params=_params(("arbitrary",)),
        name="short_conv",
    )(x, gain, w_in, conv_w, w_out)


def _kv_kernel(mem_ref, g_ref, wkv_ref, kv_ref):
    mem_n = _rms(mem_ref[...], g_ref[...]).astype(BF16)
    kv_ref[...] = _dot(mem_n, wkv_ref[...].astype(BF16)).astype(BF16)


def _kv_proj(mem, gain, wkv, layer):
    mlen, d = mem.shape
    n = wkv.shape[2]
    return pl.pallas_call(
        _kv_kernel,
        out_shape=jax.ShapeDtypeStruct((mlen, n), BF16),
        grid=(n // KV_TN,),
        in_specs=[
            pl.BlockSpec((mlen, d), lambda j: (0, 0)),
            pl.BlockSpec((1, d), lambda j: (0, 0)),
            pl.BlockSpec((None, d, KV_TN), lambda j: (layer, 0, j)),
        ],
        out_specs=pl.BlockSpec((mlen, KV_TN), lambda j: (0, j)),
        compiler_params=_params(("parallel",)),
        name="kv_proj",
    )(mem, gain, wkv)


def _xattn_kernel(x_ref, xn_ref, g_ref, wq_hbm, k_ref, v_ref, wo_hbm, o_ref,
                  wq_ref, wo_ref, stage, sem, qa_ref, qb_ref, oa_ref, ob_ref, *, layer):
    hd = XATTN_HEAD_DIM
    tm = ATT_TM

    def normed(x):
        return _rms(x, g_ref[...]).astype(BF16)

    def attend(q_ref, x, ob_scr, h_next, q_next):
        for hh in range(XATTN_HEADS):
            cols = slice(hh * hd, (hh + 1) * hd)
            s = lax.dot_general(q_ref[:, cols], k_ref[:, cols], (((1,), (1,)), ((), ())),
                                preferred_element_type=F32) * (hd ** -0.5)
            q_next[:, cols] = _dot(h_next, wq_ref[:, cols]).astype(BF16)
            p = jnp.exp(s - jnp.max(s, axis=-1, keepdims=True))
            p = p / jnp.sum(p, axis=-1, keepdims=True)
            ob_scr[:, cols] = _dot(p.astype(BF16), v_ref[:, cols]).astype(BF16)
        return x + _dot(ob_scr[...], wo_ref[...])

    @pl.when(pl.program_id(0) == 0)
    def _():
        _load_weight(wq_hbm.at[layer], wq_ref, stage, sem)
        _load_weight(wo_hbm.at[layer], wo_ref, stage, sem)
        qa_ref[...] = _dot(normed(x_ref[0:tm, :]), wq_ref[...]).astype(BF16)

    x0 = x_ref[0:tm, :]
    x1 = x_ref[tm:2 * tm, :]
    o_ref[0:tm, :] = attend(qa_ref, x0, oa_ref, normed(x1), qb_ref)
    o_ref[tm:2 * tm, :] = attend(qb_ref, x1, ob_ref, normed(xn_ref[...]), qa_ref)


def _xattn(x, gain, wq, kv, wo, layer):
    m, d = x.shape
    mlen = kv.shape[0]
    n = m // (2 * ATT_TM)
    row = lambda i: (i, 0)
    w_scratch, stage, sem = _stage_scratch((d, d))
    half = pltpu.VMEM((ATT_TM, d), BF16)
    return pl.pallas_call(
        functools.partial(_xattn_kernel, layer=layer),
        out_shape=jax.ShapeDtypeStruct((m, d), F32),
        grid=(n,),
        in_specs=[
            pl.BlockSpec((2 * ATT_TM, d), row),
            pl.BlockSpec((ATT_TM, d), lambda i: (jnp.minimum(2 * i + 2, 2 * n - 1), 0)),
            _resident((1, d)),
            _HBM,
            pl.BlockSpec((mlen, d), lambda i: (0, 0), pipeline_mode=pl.Buffered(1)),
            pl.BlockSpec((mlen, d), lambda i: (0, 1), pipeline_mode=pl.Buffered(1)),
            _HBM,
        ],
        out_specs=pl.BlockSpec((2 * ATT_TM, d), row),
        scratch_shapes=[w_scratch, w_scratch, stage, sem, half, half, half, half],
        compiler_params=_params(("arbitrary",)),
        name="xattn",
    )(x, x, gain, wq, kv, kv, wo)


def kernel(x, mem, ffn1_norm, ffn1_w13, ffn1_w2, mix_norm, gmlp_w_in, gmlp_ln_g, gmlp_ln_b, gmlp_w_s, gmlp_b_s, gmlp_w_out, conv_w_in, conv_w, conv_w_out, xattn_norm, mem_norm, xattn_wq, xattn_wkv, xattn_wo, ffn2_norm, ffn2_w13, ffn2_w2, final_norm):
    bsz, seq, d = x.shape
    assert (bsz, seq, d) == (1, SEQ, D_MODEL), "conv carry assumes one sequence laid out along rows"
    xs = x.reshape(seq, d)
    mem2 = mem.reshape(MEM_LEN, d)
    vec = lambda v: v.reshape(1, -1)

    for i in range(DEPTH):
        xs = _ffn(xs, vec(ffn1_norm[i]), ffn1_w13, ffn1_w2, i)
        j = i // 2
        if i % 2 == 0:
            xs = _gmlp(xs, vec(mix_norm[i]), gmlp_w_in, vec(gmlp_ln_g[j]), vec(gmlp_ln_b[j]),
                       gmlp_w_s[j], gmlp_b_s[j], gmlp_w_out, j)
        else:
            xs = _short_conv(xs, vec(mix_norm[i]), conv_w_in, conv_w[j], conv_w_out, j)
        kv = _kv_proj(mem2, vec(mem_norm[i]), xattn_wkv, i)
        xs = _xattn(xs, vec(xattn_norm[i]), xattn_wq, kv, xattn_wo, i)
        last = i == DEPTH - 1
        xs = _ffn(xs, vec(ffn2_norm[i]), ffn2_w13, ffn2_w2, i,
                  final_gain=vec(final_norm) if last else None)
    return xs.reshape(bsz, seq, d)
```

```python
import functools

import jax
import jax.numpy as jnp
from jax import lax
from jax.experimental import pallas as pl
from jax.experimental.pallas import tpu as pltpu

D_MODEL = 2048
SEQ = 8192
DEPTH = 2
MEM_LEN = 256
D_FF = 5632
CHUNK = 128
GMLP_WIDTH = 2048
GMLP_GROUPS = 8
GMLP_GROUP_DIM = GMLP_WIDTH // GMLP_GROUPS
CONV_WIDTH = 3
XATTN_HEADS = 4
XATTN_HEAD_DIM = D_MODEL // XATTN_HEADS
RMS_EPS = 1e-6
LN_EPS = 1e-5

F32 = jnp.float32
BF16 = jnp.bfloat16

V7X_VMEM_LIMIT_BYTES = 60 * 1024 * 1024
SUBLANES = 8

FFN_TM = 1024
FFN_TF = 512
FFN_FIRST_ROWS = FFN_TM
NORM_ROWS = 256
MIX_TM = 256
GMLP_TM = 512
ATT_TM = 256
CONV_TN = 512
KV_TN = 1024
STAGE_BYTES = 2 * 1024 * 1024


def _rms(x, g):
    return x * lax.rsqrt(jnp.mean(x * x, axis=-1, keepdims=True) + RMS_EPS) * g


def _dot(a, b):
    return jnp.dot(a, b, preferred_element_type=F32)


def _params(semantics):
    return pltpu.CompilerParams(dimension_semantics=semantics,
                                vmem_limit_bytes=V7X_VMEM_LIMIT_BYTES)


def _resident(shape):
    return pl.BlockSpec(shape, lambda *_: (0,) * len(shape), pipeline_mode=pl.Buffered(1))


_HBM = pl.BlockSpec(memory_space=pl.ANY)


def _stage_rows(w_shape):
    rows = STAGE_BYTES // (w_shape[1] * 4) // 16 * 16
    while w_shape[0] % rows:
        rows -= 16
    return rows


def _stage_scratch(w_shape):
    return [pltpu.VMEM(w_shape, BF16),
            pltpu.VMEM((2, _stage_rows(w_shape), w_shape[1]), F32),
            pltpu.SemaphoreType.DMA((2,))]


def _load_weight(w_hbm, w_vmem, stage, sem, col0=0):
    rows, width = stage.shape[1], stage.shape[2]
    n = w_hbm.shape[0] // rows
    cols = slice(col0, col0 + width)

    def copy(c, slot):
        return pltpu.make_async_copy(w_hbm.at[pl.ds(c * rows, rows), cols], stage.at[slot], sem.at[slot])

    copy(0, 0).start()

    def body(c, carry):
        slot = lax.rem(c, 2)

        @pl.when(c + 1 < n)
        def _():
            copy(c + 1, 1 - slot).start()

        copy(c, slot).wait()
        w_vmem[pl.ds(pl.multiple_of(c * rows, rows), rows), cols] = stage[slot].astype(BF16)
        return carry

    lax.fori_loop(0, n, body, 0)


def _swiglu(h, w13, w2):
    tf = w13.shape[1] // 2
    gu = _dot(h, w13)
    gate = gu[:, :tf]
    up = gu[:, tf:]
    act = (gate * (0.5 / (1.0 + jnp.exp(-gate))) * up).astype(BF16)
    return _dot(act, w2)


def _ffn_first_kernel(x_hbm, g_ref, w1_ref, w3_ref, w2_ref, fg_ref, o_ref, w13b_ref, w2b_ref,
                      h_ref, sem, *, final_norm):
    j = pl.program_id(0)

    @pl.when(j == 0)
    def _():
        copy = pltpu.make_async_copy(x_hbm.at[pl.ds(0, FFN_FIRST_ROWS)], o_ref, sem)
        copy.start()
        copy.wait()
        for r in range(0, FFN_FIRST_ROWS, NORM_ROWS):
            h_ref[r:r + NORM_ROWS, :] = _rms(o_ref[r:r + NORM_ROWS, :], g_ref[...]).astype(BF16)

    w13 = jnp.concatenate([w1_ref[...].astype(BF16), w3_ref[...].astype(BF16)], axis=1)
    w2 = w2_ref[...].astype(BF16)
    w13b_ref[...] = w13
    w2b_ref[...] = w2
    o_ref[...] += _swiglu(h_ref[...], w13, w2)

    if final_norm:
        @pl.when(j == pl.num_programs(0) - 1)
        def _():
            for r in range(0, FFN_FIRST_ROWS, NORM_ROWS):
                o_ref[r:r + NORM_ROWS, :] = _rms(o_ref[r:r + NORM_ROWS, :], fg_ref[...])


def _ffn_rest_kernel(x_hbm, g_ref, w13_ref, w2_ref, fg_ref, y0_hbm, o_ref, h_ref, xs_ref, sem, xsem,
                     *, nf, n_first, n_blocks, final_norm):
    tf = FFN_TF
    n_pair = nf // 2
    steps = n_pair + nf % 2
    s = pl.program_id(0)
    t = jnp.maximum(s - n_first, 0)
    j = lax.rem(t, steps)
    blk = n_first + t // steps

    def x_copy(b):
        return pltpu.make_async_copy(x_hbm.at[pl.ds(pl.multiple_of(b * FFN_TM, FFN_TM), FFN_TM)], xs_ref, xsem)

    @pl.when(s < n_first)
    def _():
        @pl.when(s == 0)
        def _():
            x_copy(n_first).start()

        copy = pltpu.make_async_copy(y0_hbm.at[pl.ds(pl.multiple_of(s * FFN_TM, FFN_TM), FFN_TM)], o_ref, sem)
        copy.start()
        copy.wait()

    @pl.when(s >= n_first)
    def _():
        @pl.when(j == 0)
        def _():
            x_copy(blk).wait()
            h_ref[...] = _rms(xs_ref[...], g_ref[...]).astype(BF16)
            o_ref[...] = xs_ref[...] + _swiglu(h_ref[...], w13_ref[0], w2_ref[0:tf, :])
            o_ref[...] += _swiglu(h_ref[...], w13_ref[1], w2_ref[tf:2 * tf, :])

            @pl.when(blk + 1 < n_blocks)
            def _():
                x_copy(blk + 1).start()

        @pl.when(jnp.logical_and(j > 0, j < n_pair))
        def _():
            o_ref[...] += _swiglu(h_ref[...], w13_ref[0], w2_ref[0:tf, :])
            o_ref[...] += _swiglu(h_ref[...], w13_ref[1], w2_ref[tf:2 * tf, :])

        if nf % 2:
            @pl.when(j == n_pair)
            def _():
                o_ref[...] += _swiglu(h_ref[...], w13_ref[0], w2_ref[0:tf, :])

        if final_norm:
            @pl.when(j == steps - 1)
            def _():
                o_ref[...] = _rms(o_ref[...], fg_ref[...])


def _ffn(x, gain, w13, w2, layer, final_gain=None):
    m, d = x.shape
    f = w2.shape[1]
    final_norm = final_gain is not None
    if final_gain is None:
        final_gain = gain
    vec = pl.BlockSpec((1, d), lambda s: (0, 0))

    nf = f // FFN_TF
    y0, w13b, w2b = pl.pallas_call(
        functools.partial(_ffn_first_kernel, final_norm=final_norm),
        out_shape=[jax.ShapeDtypeStruct((FFN_FIRST_ROWS, d), F32),
                   jax.ShapeDtypeStruct((nf, d, 2 * FFN_TF), BF16),
                   jax.ShapeDtypeStruct((f, d), BF16)],
        grid=(nf,),
        in_specs=[
            _HBM,
            vec,
            pl.BlockSpec((None, d, FFN_TF), lambda j: (layer, 0, j)),
            pl.BlockSpec((None, d, FFN_TF), lambda j: (layer, 0, j + nf)),
            pl.BlockSpec((None, FFN_TF, d), lambda j: (layer, j, 0)),
            vec,
        ],
        out_specs=[
            pl.BlockSpec((FFN_FIRST_ROWS, d), lambda j: (0, 0)),
            pl.BlockSpec((None, d, 2 * FFN_TF), lambda j: (j, 0, 0)),
            pl.BlockSpec((FFN_TF, d), lambda j: (j, 0)),
        ],
        scratch_shapes=[pltpu.VMEM((FFN_FIRST_ROWS, d), BF16), pltpu.SemaphoreType.DMA(())],
        compiler_params=_params(("arbitrary",)),
        name="ffn_first",
    )(x, gain, w13, w13, w2, final_gain)

    n_first = FFN_FIRST_ROWS // FFN_TM
    n_blocks = m // FFN_TM
    steps = nf // 2 + nf % 2
    assert n_first == 1
    rest = lambda s: jnp.maximum(s - n_first, 0)
    out_row = lambda s: (jnp.where(s < n_first, s, n_first + rest(s) // steps), 0)
    w_pair = lambda s: lax.rem(rest(s), steps)
    return pl.pallas_call(
        functools.partial(_ffn_rest_kernel, nf=nf, n_first=n_first, n_blocks=n_blocks, final_norm=final_norm),
        out_shape=jax.ShapeDtypeStruct((m, d), F32),
        grid=(n_first + (n_blocks - n_first) * steps,),
        in_specs=[
            _HBM,
            vec,
            pl.BlockSpec((2, d, 2 * FFN_TF), lambda s: (w_pair(s), 0, 0)),
            pl.BlockSpec((2 * FFN_TF, d), lambda s: (w_pair(s), 0)),
            vec,
            _HBM,
        ],
        out_specs=pl.BlockSpec((FFN_TM, d), out_row),
        scratch_shapes=[pltpu.VMEM((FFN_TM, d), BF16), pltpu.VMEM((FFN_TM, d), F32),
                        pltpu.SemaphoreType.DMA(()), pltpu.SemaphoreType.DMA(())],
        compiler_params=_params(("arbitrary",)),
        name="ffn_rest",
    )(x, gain, w13b, w2b, final_gain, y0)


def _gmlp_kernel(x_ref, g_ref, win_hbm, lng_ref, lnb_ref, ws_ref, bs_ref, wout_hbm, o_ref,
                 win_ref, wout_ref, u_ref, sem, v_ref, uf_ref, *, layer):
    e = GMLP_WIDTH
    half = GMLP_TM // 2

    @pl.when(pl.program_id(0) == 0)
    def _():
        _load_weight(win_hbm.at[layer], win_ref, u_ref, sem, col0=0)
        _load_weight(win_hbm.at[layer], win_ref, u_ref, sem, col0=e)
        _load_weight(wout_hbm.at[layer], wout_ref, u_ref, sem)

    x = x_ref[...]
    h = _rms(x, g_ref[...]).astype(BF16)

    def gelu(z):
        return 0.5 * z * (1.0 + lax.erf(z * (2.0 ** -0.5)))

    v = gelu(_dot(h, win_ref[:, e:]))
    u = gelu(_dot(h, win_ref[:, :e]))
    u_ref[0] = u[:half]
    u_ref[1] = u[half:]
    mu = jnp.mean(v, axis=-1, keepdims=True)
    vc = v - mu
    var = jnp.mean(vc * vc, axis=-1, keepdims=True)
    v_ref[...] = (vc * lax.rsqrt(var + LN_EPS) * lng_ref[...] + lnb_ref[...]).astype(BF16)

    t_idx = lax.broadcasted_iota(jnp.int32, (CHUNK, CHUNK), 0)
    s_idx = lax.broadcasted_iota(jnp.int32, (CHUNK, CHUNK), 1)
    causal = t_idx >= s_idx
    for g in range(GMLP_GROUPS):
        w = jnp.where(causal, ws_ref[g], 0.0).astype(BF16)
        cols = slice(g * GMLP_GROUP_DIM, (g + 1) * GMLP_GROUP_DIM)
        for c in range(GMLP_TM // CHUNK):
            rows = slice(c * CHUNK, (c + 1) * CHUNK)
            r0 = (c * CHUNK) % half
            f = _dot(w, v_ref[rows, cols]) + bs_ref[g]
            uf_ref[rows, cols] = (u_ref[(c * CHUNK) // half, r0:r0 + CHUNK, cols] * f).astype(BF16)
    o_ref[...] = x + _dot(uf_ref[...], wout_ref[...])


def _gmlp(x, gain, w_in, ln_g, ln_b, w_s, b_s, w_out, layer):
    m, d = x.shape
    e = GMLP_WIDTH
    row = lambda i: (i, 0)
    bias = jnp.broadcast_to(b_s[:, :, None], (GMLP_GROUPS, CHUNK, GMLP_GROUP_DIM))
    assert d == e, "one staging width serves both weights"
    return pl.pallas_call(
        functools.partial(_gmlp_kernel, layer=layer),
        out_shape=jax.ShapeDtypeStruct((m, d), F32),
        grid=(m // GMLP_TM,),
        in_specs=[
            pl.BlockSpec((GMLP_TM, d), row),
            _resident((1, d)),
            _HBM,
            _resident((1, e)),
            _resident((1, e)),
            _resident((GMLP_GROUPS, CHUNK, CHUNK)),
            _resident((GMLP_GROUPS, CHUNK, GMLP_GROUP_DIM)),
            _HBM,
        ],
        out_specs=pl.BlockSpec((GMLP_TM, d), row),
        scratch_shapes=[pltpu.VMEM((d, 2 * e), BF16),
                        pltpu.VMEM((e, d), BF16),
                        pltpu.VMEM((2, GMLP_TM // 2, e), F32),
                        pltpu.SemaphoreType.DMA((2,)),
                        pltpu.VMEM((GMLP_TM, e), BF16),
                        pltpu.VMEM((GMLP_TM, e), BF16)],
        compiler_params=_params(("arbitrary",)),
        name="gmlp",
    )(x, gain, w_in, ln_g, ln_b, w_s, bias, w_out)


def _conv_kernel(x_ref, g_ref, win_hbm, cw_ref, wout_hbm, o_ref,
                 win_ref, win_stage, win_sem, wout_ref, wout_stage, wout_sem, z_ref, gb_ref, *, layer):
    d = D_MODEL
    tm = MIX_TM
    i = pl.program_id(0)

    @pl.when(i == 0)
    def _():
        _load_weight(win_hbm.at[layer], win_ref, win_stage, win_sem)
        _load_weight(wout_hbm.at[layer], wout_ref, wout_stage, wout_sem)
        z_ref[0:SUBLANES, :] = jnp.zeros((SUBLANES, d), F32)

    @pl.when(i > 0)
    def _():
        z_ref[0:SUBLANES, :] = z_ref[tm:tm + SUBLANES, :]

    x = x_ref[...]
    h = _rms(x, g_ref[...]).astype(BF16)
    for jb in range(d // CONV_TN):
        lo = jb * CONV_TN
        cols = slice(lo, lo + CONV_TN)
        gate_b = _dot(h, win_ref[:, lo:lo + CONV_TN])
        gate_c = _dot(h, win_ref[:, d + lo:d + lo + CONV_TN])
        val = _dot(h, win_ref[:, 2 * d + lo:2 * d + lo + CONV_TN])
        z = gate_c * val
        z_ref[SUBLANES:SUBLANES + tm, cols] = z
        conv = (cw_ref[0:1, cols] * z_ref[SUBLANES - 2:SUBLANES - 2 + tm, cols]
                + cw_ref[1:2, cols] * z_ref[SUBLANES - 1:SUBLANES - 1 + tm, cols]
                + cw_ref[2:3, cols] * z)
        gb_ref[:, cols] = (gate_b * conv).astype(BF16)
    o_ref[...] = x + _dot(gb_ref[...], wout_ref[...])


def _short_conv(x, gain, w_in, conv_w, w_out, layer):
    m, d = x.shape
    row = lambda i: (i, 0)
    return pl.pallas_call(
        functools.partial(_conv_kernel, layer=layer),
        out_shape=jax.ShapeDtypeStruct((m, d), F32),
        grid=(m // MIX_TM,),
        in_specs=[
            pl.BlockSpec((MIX_TM, d), row),
            _resident((1, d)),
            _HBM,
            _resident((CONV_WIDTH, d)),
            _HBM,
        ],
        out_specs=pl.BlockSpec((MIX_TM, d), row),
        scratch_shapes=[*_stage_scratch((d, 3 * d)),
                        *_stage_scratch((d, d)),
                        pltpu.VMEM((MIX_TM + SUBLANES, d), F32),
                        pltpu.VMEM((MIX_TM, d), BF16)],
        compiler_params=_params(("arbitrary",)),
        name="short_conv",
    )(x, gain, w_in, conv_w, w_out)


def _kv_kernel(mem_ref, g_ref, wkv_ref, kv_ref):
    mem_n = _rms(mem_ref[...], g_ref[...]).astype(BF16)
    kv_ref[...] = _dot(mem_n, wkv_ref[...].astype(BF16)).astype(BF16)


def _kv_proj(mem, gains, wkv):
    mlen, d = mem.shape
    depth, _, n = wkv.shape
    return pl.pallas_call(
        _kv_kernel,
        out_shape=jax.ShapeDtypeStruct((depth, mlen, n), BF16),
        grid=(depth, n // KV_TN),
        in_specs=[
            pl.BlockSpec((mlen, d), lambda l, j: (0, 0)),
            pl.BlockSpec((None, 1, d), lambda l, j: (l, 0, 0)),
            pl.BlockSpec((None, d, KV_TN), lambda l, j: (l, 0, j)),
        ],
        out_specs=pl.BlockSpec((None, mlen, KV_TN), lambda l, j: (l, 0, j)),
        compiler_params=_params(("parallel", "parallel")),
        name="kv_proj",
    )(mem, gains.reshape(depth, 1, d), wkv)


def _xattn_kernel(x_ref, xn_ref, g_ref, wq_hbm, k_ref, v_ref, wo_hbm, o_ref,
                  wq_ref, wo_ref, stage, sem, qa_ref, qb_ref, oa_ref, ob_ref, *, layer):
    hd = XATTN_HEAD_DIM
    tm = ATT_TM

    def normed(x):
        return _rms(x, g_ref[...]).astype(BF16)

    def attend(q_ref, x, ob_scr, h_next, q_next):
        for hh in range(XATTN_HEADS):
            cols = slice(hh * hd, (hh + 1) * hd)
            s = lax.dot_general(q_ref[:, cols], k_ref[:, cols], (((1,), (1,)), ((), ())),
                                preferred_element_type=F32) * (hd ** -0.5)
            q_next[:, cols] = _dot(h_next, wq_ref[:, cols]).astype(BF16)
            p = jnp.exp(s - jnp.max(s, axis=-1, keepdims=True))
            p = p / jnp.sum(p, axis=-1, keepdims=True)
            ob_scr[:, cols] = _dot(p.astype(BF16), v_ref[:, cols]).astype(BF16)
        return x + _dot(ob_scr[...], wo_ref[...])

    @pl.when(pl.program_id(0) == 0)
    def _():
        _load_weight(wq_hbm.at[layer], wq_ref, stage, sem)
        _load_weight(wo_hbm.at[layer], wo_ref, stage, sem)
        qa_ref[...] = _dot(normed(x_ref[0:tm, :]), wq_ref[...]).astype(BF16)

    x0 = x_ref[0:tm, :]
    x1 = x_ref[tm:2 * tm, :]
    o_ref[0:tm, :] = attend(qa_ref, x0, oa_ref, normed(x1), qb_ref)
    o_ref[tm:2 * tm, :] = attend(qb_ref, x1, ob_ref, normed(xn_ref[...]), qa_ref)


def _xattn(x, gain, wq, kv, wo, layer):
    m, d = x.shape
    mlen = kv.shape[1]
    n = m // (2 * ATT_TM)
    row = lambda i: (i, 0)
    w_scratch, stage, sem = _stage_scratch((d, d))
    half = pltpu.VMEM((ATT_TM, d), BF16)
    return pl.pallas_call(
        functools.partial(_xattn_kernel, layer=layer),
        out_shape=jax.ShapeDtypeStruct((m, d), F32),
        grid=(n,),
        in_specs=[
            pl.BlockSpec((2 * ATT_TM, d), row),
            pl.BlockSpec((ATT_TM, d), lambda i: (jnp.minimum(2 * i + 2, 2 * n - 1), 0)),
            _resident((1, d)),
            _HBM,
            pl.BlockSpec((None, mlen, d), lambda i: (layer, 0, 0), pipeline_mode=pl.Buffered(1)),
            pl.BlockSpec((None, mlen, d), lambda i: (layer, 0, 1), pipeline_mode=pl.Buffered(1)),
            _HBM,
        ],
        out_specs=pl.BlockSpec((2 * ATT_TM, d), row),
        scratch_shapes=[w_scratch, w_scratch, stage, sem, half, half, half, half],
        compiler_params=_params(("arbitrary",)),
        name="xattn",
    )(x, x, gain, wq, kv, kv, wo)


def kernel(x, mem, ffn1_norm, ffn1_w13, ffn1_w2, mix_norm, gmlp_w_in, gmlp_ln_g, gmlp_ln_b, gmlp_w_s, gmlp_b_s, gmlp_w_out, conv_w_in, conv_w, conv_w_out, xattn_norm, mem_norm, xattn_wq, xattn_wkv, xattn_wo, ffn2_norm, ffn2_w13, ffn2_w2, final_norm):
    bsz, seq, d = x.shape
    assert (bsz, seq, d) == (1, SEQ, D_MODEL), "conv carry assumes one sequence laid out along rows"
    xs = x.reshape(seq, d)
    mem2 = mem.reshape(MEM_LEN, d)
    vec = lambda v: v.reshape(1, -1)

    kv = _kv_proj(mem2, mem_norm, xattn_wkv)
    for i in range(DEPTH):
        xs = _ffn(xs, vec(ffn1_norm[i]), ffn1_w13, ffn1_w2, i)
        j = i // 2
        if i % 2 == 0:
            xs = _gmlp(xs, vec(mix_norm[i]), gmlp_w_in, vec(gmlp_ln_g[j]), vec(gmlp_ln_b[j]),
                       gmlp_w_s[j], gmlp_b_s[j], gmlp_w_out, j)
        else:
            xs = _short_conv(xs, vec(mix_norm[i]), conv_w_in, conv_w[j], conv_w_out, j)
        xs = _xattn(xs, vec(xattn_norm[i]), xattn_wq, kv, xattn_wo, i)
        last = i == DEPTH - 1
        xs = _ffn(xs, vec(ffn2_norm[i]), ffn2_w13, ffn2_w2, i,
                  final_gain=vec(final_norm) if last else None)
    return xs.reshape(bsz, seq, d)
```
